```python
import math
import jax, jax.numpy as jnp
from jax import lax
import numpy as np

D_MODEL = 2048
BATCH = 8
SEQ = 4096
DEPTH = 4

CTX_LEN = 256
GRID_W = 64

DIFF_HEADS = 8
DIFF_HD = 64
DIFF_W = DIFF_HEADS * 2 * DIFF_HD
Q_BLOCK = 128
ROPE_BASE = 10000.0
ROPE_FREQS = DIFF_HD // 4
CONV_W = 1024
CONV_K = 3
RNN_W = 1024
RNN_BLOCKS = 8
RNN_BW = RNN_W // RNN_BLOCKS
RNN_CONV_K = 4
RG_C = 8.0
N_BRANCH = 3
IN_COLS = 3 * DIFF_W + 3 * CONV_W + 2 * RNN_W + N_BRANCH * D_MODEL
FFN_HIDDEN = -(-8 * D_MODEL // (3 * 256)) * 256
N_MOD = 6
EPS = 1e-6

kernel_name = 'hybrid_dit_diffattn_shortconv_rglru'


def rmsnorm(x, g):
    xf = x.astype(jnp.float32)
    y = xf * lax.rsqrt(jnp.mean(xf * xf, axis=-1, keepdims=True) + EPS)
    return (y * g.astype(jnp.float32)).astype(x.dtype)


def modulate(x, shift, scale):
    return x * (1.0 + scale) + shift


def axial_rope_tables(n_tokens):
    rows = n_tokens // GRID_W
    row = jnp.repeat(jnp.arange(rows, dtype=jnp.float32), GRID_W)
    col = jnp.tile(jnp.arange(GRID_W, dtype=jnp.float32), rows)
    inv = ROPE_BASE ** (-jnp.arange(ROPE_FREQS, dtype=jnp.float32) / ROPE_FREQS)
    ang = jnp.stack([row[:, None] * inv, col[:, None] * inv], axis=1)
    return jnp.cos(ang), jnp.sin(ang)


def apply_rope(x, cos, sin):
    xf = x.astype(jnp.float32).reshape(*x.shape[:-1], 2, 2, ROPE_FREQS)
    x1, x2 = xf[..., 0, :], xf[..., 1, :]
    cs, sn = cos[None, :, None, None], sin[None, :, None, None]
    out = jnp.stack([x1 * cs - x2 * sn, x2 * cs + x1 * sn], axis=-2)
    return out.reshape(x.shape).astype(x.dtype)


def diff_attend(q, k, v, lam, lam_init, subln_g):
    s = jnp.einsum('bqhcd,bkhcd->bhcqk', q, k, preferred_element_type=jnp.float32) * (DIFF_HD ** -0.5)
    p = jax.nn.softmax(s, axis=-1)
    w = p[:, :, 0] - lam * p[:, :, 1]
    o = jnp.einsum('bhqk,bkhe->bqhe', w.astype(v.dtype), v)
    o = rmsnorm(o, subln_g) * (1.0 - lam_init)
    return o.reshape(*o.shape[:2], DIFF_W)


def diff_attend_blocked(q, k, v, lam, lam_init, subln_g):
    b, s = q.shape[:2]
    nb = s // Q_BLOCK
    qb = jnp.swapaxes(q.reshape(b, nb, Q_BLOCK, *q.shape[2:]), 0, 1)
    ob = lax.map(lambda qi: diff_attend(qi, k, v, lam, lam_init, subln_g), qb)
    return jnp.swapaxes(ob, 0, 1).reshape(b, s, DIFF_W)


def dwconv(x, w, bias, pad):
    y = lax.conv_general_dilated(x, w[:, None, :].astype(x.dtype), window_strides=(1,), padding=[pad],
                                 dimension_numbers=('NWC', 'WIO', 'NWC'), feature_group_count=x.shape[-1])
    return y + bias


def rglru_coeffs(xr, wa, ba, wx, bx, lam):
    xf = xr.astype(jnp.float32)
    xb = xf.reshape(*xf.shape[:2], RNN_BLOCKS, RNN_BW)
    r = jax.nn.sigmoid(jnp.einsum('bsnj,njk->bsnk', xb, wa.astype(jnp.float32)).reshape(xf.shape) + ba.astype(jnp.float32))
    i = jax.nn.sigmoid(jnp.einsum('bsnj,njk->bsnk', xb, wx.astype(jnp.float32)).reshape(xf.shape) + bx.astype(jnp.float32))
    log_a = -RG_C * r * jax.nn.softplus(-lam.astype(jnp.float32))
    return jnp.exp(log_a), jnp.sqrt(-jnp.expm1(2.0 * log_a)) * (i * xf)


def linear_scan(a, g, reverse, h0=None):
    if h0 is not None:
        first = -1 if reverse else 0
        g = g.at[:, first].add(a[:, first] * h0)

    def combine(left, right):
        a_l, g_l = left
        a_r, g_r = right
        return a_l * a_r, a_r * g_l + g_r

    _, h = lax.associative_scan(combine, (a, g), axis=1, reverse=reverse)
    return h


def split_proj(p):
    o1 = 3 * DIFF_W
    o2 = o1 + 3 * CONV_W
    idx = [DIFF_W, 2 * DIFF_W, o1, o1 + CONV_W, o1 + 2 * CONV_W, o2, o2 + RNN_W, o2 + 2 * RNN_W]
    return jnp.split(p, idx, axis=-1)


def merge(gate_pre, ya, yb, yc, b_merge, w_branch_a, w_branch_b, w_branch_c, w_o):
    g = jax.nn.sigmoid(gate_pre.reshape(*gate_pre.shape[:-1], N_BRANCH, D_MODEL) + b_merge)
    m = g[..., 0, :] * (ya @ w_branch_a) + g[..., 1, :] * (yb @ w_branch_b) + g[..., 2, :] * (yc @ w_branch_c)
    return m @ w_o


def token_mixers(u_lat, u_ctx, cos, sin, layer, need_ctx, w_in, diff_lambda, diff_subln, conv_w, conv_b,
                 rnn_conv_w, rnn_conv_b, rg_wa, rg_ba, rg_wx, rg_bx, rg_lambda, b_merge,
                 w_branch_a, w_branch_b, w_branch_c, w_o):
    b, s, _ = u_lat.shape
    n_ctx = u_ctx.shape[1]
    lam_init = 0.8 - 0.6 * math.exp(-0.3 * layer)
    lq1, lk1, lq2, lk2 = diff_lambda.astype(jnp.float32)
    lam = jnp.exp(jnp.sum(lq1 * lk1)) - jnp.exp(jnp.sum(lq2 * lk2)) + lam_init

    pl = split_proj(u_lat @ w_in)
    pc = split_proj(u_ctx @ w_in)

    q_l = apply_rope(pl[0].reshape(b, s, DIFF_HEADS, 2, DIFF_HD), cos, sin)
    k_l = apply_rope(pl[1].reshape(b, s, DIFF_HEADS, 2, DIFF_HD), cos, sin)
    v_l = pl[2].reshape(b, s, DIFF_HEADS, 2 * DIFF_HD)
    k_c = pc[1].reshape(b, n_ctx, DIFF_HEADS, 2, DIFF_HD)
    v_c = pc[2].reshape(b, n_ctx, DIFF_HEADS, 2 * DIFF_HD)
    k_all = jnp.concatenate([k_c, k_l], axis=1)
    v_all = jnp.concatenate([v_c, v_l], axis=1)
    ya_l = diff_attend_blocked(q_l, k_all, v_all, lam, lam_init, diff_subln)

    yb_l = pl[5] * dwconv(pl[4] * pl[3], conv_w, conv_b, (1, 1))

    xr_l = dwconv(pl[7], rnn_conv_w, rnn_conv_b, (2, 1))
    xr_c = dwconv(pc[7], rnn_conv_w, rnn_conv_b, (2, 1))
    hl_dirs, hc_dirs = [], []
    for d, rev in enumerate((False, True)):
        a_c, g_c = rglru_coeffs(xr_c, rg_wa[d], rg_ba[d], rg_wx[d], rg_bx[d], rg_lambda[d])
        hc = linear_scan(a_c, g_c, rev)
        h_final = hc[:, 0] if rev else hc[:, -1]
        a_l, g_l = rglru_coeffs(xr_l, rg_wa[d], rg_ba[d], rg_wx[d], rg_bx[d], rg_lambda[d])
        hl_dirs.append(linear_scan(a_l, g_l, rev, h_final))
        hc_dirs.append(hc)
    yc_l = jax.nn.gelu(pl[6]) * (hl_dirs[0] + hl_dirs[1]).astype(u_lat.dtype)

    out_l = merge(pl[8], ya_l, yb_l, yc_l, b_merge, w_branch_a, w_branch_b, w_branch_c, w_o)
    if not need_ctx:
        return out_l, None

    q_c = pc[0].reshape(b, n_ctx, DIFF_HEADS, 2, DIFF_HD)
    ya_c = diff_attend(q_c, k_c, v_c, lam, lam_init, diff_subln)
    yb_c = pc[5] * dwconv(pc[4] * pc[3], conv_w, conv_b, (1, 1))
    yc_c = jax.nn.gelu(pc[6]) * (hc_dirs[0] + hc_dirs[1]).astype(u_ctx.dtype)
    out_c = merge(pc[8], ya_c, yb_c, yc_c, b_merge, w_branch_a, w_branch_b, w_branch_c, w_o)
    return out_l, out_c


def swiglu(u, wg, wu, wd):
    return (jax.nn.silu(u @ wg) * (u @ wu)) @ wd


def setup_inputs(seed: int = 0) -> dict:
    key = jax.random.key(seed)
    ks = list(jax.random.split(key, 40))
    counter = [0]

    def nrm(shape, scale):
        k = ks[counter[0]]
        counter[0] += 1
        return jax.random.normal(k, shape, jnp.float32) * scale

    D = D_MODEL
    inp = {}
    inp['x'] = nrm((BATCH, SEQ, D), 1.0)
    inp['c'] = nrm((BATCH, D), 1.0)
    inp['ctx'] = nrm((BATCH, CTX_LEN, D), 1.0)
    inp['c_ctx'] = nrm((D,), 1.0)
    inp['w_ada'] = nrm((DEPTH, D, N_MOD * D), 0.5 * D ** -0.5)
    inp['b_ada'] = nrm((DEPTH, N_MOD, D), 0.02)
    inp['g_pre_mix'] = 1.0 + nrm((DEPTH, D), 0.05)
    inp['g_post_mix'] = 1.0 + nrm((DEPTH, D), 0.05)
    inp['g_pre_ffn'] = 1.0 + nrm((DEPTH, D), 0.05)
    inp['g_post_ffn'] = 1.0 + nrm((DEPTH, D), 0.05)
    inp['w_in'] = nrm((DEPTH, D, IN_COLS), D ** -0.5)
    inp['diff_lambda'] = nrm((DEPTH, 4, DIFF_HD), 0.1)
    inp['diff_subln'] = 1.0 + nrm((DEPTH, 2 * DIFF_HD), 0.05)
    inp['conv_w'] = nrm((DEPTH, CONV_K, CONV_W), CONV_K ** -0.5)
    inp['conv_b'] = nrm((DEPTH, CONV_W), 0.02)
    inp['rnn_conv_w'] = nrm((DEPTH, RNN_CONV_K, RNN_W), RNN_CONV_K ** -0.5)
    inp['rnn_conv_b'] = nrm((DEPTH, RNN_W), 0.02)
    inp['rg_wa'] = nrm((DEPTH, 2, RNN_BLOCKS, RNN_BW, RNN_BW), RNN_BW ** -0.5)
    inp['rg_ba'] = nrm((DEPTH, 2, RNN_W), 0.02)
    inp['rg_wx'] = nrm((DEPTH, 2, RNN_BLOCKS, RNN_BW, RNN_BW), RNN_BW ** -0.5)
    inp['rg_bx'] = nrm((DEPTH, 2, RNN_W), 0.02)
    u = jax.random.uniform(ks[counter[0]], (DEPTH, 2, RNN_W), jnp.float32, 0.9, 0.999)
    counter[0] += 1
    a0 = u ** (1.0 / RG_C)
    inp['rg_lambda'] = jnp.log(a0) - jnp.log1p(-a0)
    inp['b_merge'] = nrm((DEPTH, N_BRANCH, D), 0.02)
    inp['w_branch_a'] = nrm((DEPTH, DIFF_W, D), DIFF_W ** -0.5)
    inp['w_branch_b'] = nrm((DEPTH, CONV_W, D), CONV_W ** -0.5)
    inp['w_branch_c'] = nrm((DEPTH, RNN_W, D), RNN_W ** -0.5)
    inp['w_o'] = nrm((DEPTH, D, D), D ** -0.5)
    inp['w_ffn_gate'] = nrm((DEPTH, D, FFN_HIDDEN), D ** -0.5)
    inp['w_ffn_up'] = nrm((DEPTH, D, FFN_HIDDEN), D ** -0.5)
    inp['w_ffn_down'] = nrm((DEPTH, FFN_HIDDEN, D), FFN_HIDDEN ** -0.5)
    return inp


def reference(x, c, ctx, c_ctx, w_ada, b_ada, g_pre_mix, g_post_mix, g_pre_ffn, g_post_ffn, w_in,
              diff_lambda, diff_subln, conv_w, conv_b, rnn_conv_w, rnn_conv_b, rg_wa, rg_ba, rg_wx, rg_bx,
              rg_lambda, b_merge, w_branch_a, w_branch_b, w_branch_c, w_o, w_ffn_gate, w_ffn_up, w_ffn_down):
    b, s, d = x.shape
    cos, sin = axial_rope_tables(s)
    sc = jax.nn.silu(c)
    scc = jax.nn.silu(c_ctx)
    h_lat, h_ctx = x, ctx
    for l in range(DEPTH):
        need_ctx = l < DEPTH - 1
        ml = (sc @ w_ada[l]).reshape(b, N_MOD, 1, d) + b_ada[l][:, None, :]
        mc = (scc @ w_ada[l]).reshape(N_MOD, 1, d) + b_ada[l][:, None, :]
        u_lat = modulate(rmsnorm(h_lat, g_pre_mix[l]), ml[:, 0], ml[:, 1])
        u_ctx = modulate(rmsnorm(h_ctx, g_pre_mix[l]), mc[0], mc[1])
        mix_l, mix_c = token_mixers(u_lat, u_ctx, cos, sin, l, need_ctx, w_in[l], diff_lambda[l], diff_subln[l],
                                    conv_w[l], conv_b[l], rnn_conv_w[l], rnn_conv_b[l], rg_wa[l], rg_ba[l],
                                    rg_wx[l], rg_bx[l], rg_lambda[l], b_merge[l], w_branch_a[l], w_branch_b[l],
                                    w_branch_c[l], w_o[l])
        h_lat = h_lat + ml[:, 2] * rmsnorm(mix_l, g_post_mix[l])
        u = modulate(rmsnorm(h_lat, g_pre_ffn[l]), ml[:, 3], ml[:, 4])
        h_lat = h_lat + ml[:, 5] * rmsnorm(swiglu(u, w_ffn_gate[l], w_ffn_up[l], w_ffn_down[l]), g_post_ffn[l])
        if need_ctx:
            h_ctx = h_ctx + mc[2] * rmsnorm(mix_c, g_post_mix[l])
            uc = modulate(rmsnorm(h_ctx, g_pre_ffn[l]), mc[3], mc[4])
            h_ctx = h_ctx + mc[5] * rmsnorm(swiglu(uc, w_ffn_gate[l], w_ffn_up[l], w_ffn_down[l]), g_post_ffn[l])
    return h_lat
```

```python
import functools
import math

import jax
import jax.numpy as jnp
from jax import lax
from jax.experimental import pallas as pl
from jax.experimental.pallas import tpu as pltpu

GRID_W = 64
ROPE_BASE = 10000.0
EPS = 1e-6
RG_C = 8.0
N_MOD = 6
N_BRANCH = 3
N_MID = 5
LANES = 128
SUBLANES = 8
VMEM_LIMIT_BYTES = 52 * 1024 * 1024

F32 = jnp.float32
BF16 = jnp.bfloat16


def _tile(n, pref):
    if n <= pref:
        return n
    t = pref - pref % SUBLANES
    while t >= SUBLANES:
        if n % t == 0:
            return t
        t -= SUBLANES
    return n


def _params(*sem):
    return pltpu.CompilerParams(dimension_semantics=sem, vmem_limit_bytes=VMEM_LIMIT_BYTES)


def _rms(x, g):
    return x * lax.rsqrt(jnp.mean(x * x, axis=-1, keepdims=True) + EPS) * g


def _silu(x):
    return x * jax.nn.sigmoid(x)


def _ada_kernel(x_ref, w_ref, b_ref, o_ref):
    sx = _silu(x_ref[...]).astype(BF16)
    o_ref[0] = jnp.dot(sx, w_ref[0].astype(BF16), preferred_element_type=F32) + b_ref[0]


def _ada(cc, w_ada, b_flat):
    depth, d, n = w_ada.shape
    rows = cc.shape[0]
    tn = _tile(n, 1024)
    return pl.pallas_call(
        _ada_kernel,
        grid=(depth, n // tn),
        in_specs=[
            pl.BlockSpec((rows, d), lambda l, j: (0, 0)),
            pl.BlockSpec((1, d, tn), lambda l, j: (l, 0, j)),
            pl.BlockSpec((1, 1, tn), lambda l, j: (l, 0, j)),
        ],
        out_specs=pl.BlockSpec((1, rows, tn), lambda l, j: (l, 0, j)),
        out_shape=jax.ShapeDtypeStruct((depth, rows, n), F32),
        compiler_params=_params("arbitrary", "arbitrary"),
        name="ada",
    )(cc, w_ada, b_flat)


def _inproj_kernel(h_ref, mod_ref, g_ref, w_ref, *rest, rope, q_scale):
    if rope:
        cos_ref, s1_ref, s2_ref, qkv_ref, mid_ref, gates_ref, xn_ref = rest
    else:
        qkv_ref, mid_ref, gates_ref, xn_ref = rest
    j = pl.program_id(2)

    @pl.when(j == 0)
    def _():
        m = mod_ref[0]
        y = _rms(h_ref[0], g_ref[...])
        xn_ref[...] = (y * (1.0 + m[1:2]) + m[0:1]).astype(BF16)

    acc = jnp.dot(xn_ref[...], w_ref[...], preferred_element_type=F32)
    tn = acc.shape[1]

    @pl.when(j < 2)
    def _():
        a = acc * jnp.where(j == 0, q_scale, 1.0)
        if rope:
            cos, s1, s2 = cos_ref[...], s1_ref[...], s2_ref[...]
            quarter = LANES // 8
            for c in range(tn // LANES):
                blk = a[:, c * LANES:(c + 1) * LANES]
                r = blk * cos + pltpu.roll(blk, quarter, 1) * s1 + pltpu.roll(blk, LANES - quarter, 1) * s2
                qkv_ref[0, :, c * LANES:(c + 1) * LANES] = r.astype(BF16)
        else:
            qkv_ref[0] = a.astype(BF16)

    @pl.when(j == 2)
    def _():
        qkv_ref[0] = acc.astype(BF16)

    @pl.when((j >= 3) & (j < 3 + N_MID))
    def _():
        mid_ref[0] = acc

    @pl.when(j >= 3 + N_MID)
    def _():
        gates_ref[0] = acc


def _inproj(h, mods, g, w, tables, *, width, q_scale):
    b, s, d = h.shape
    n = w.shape[1]
    tn = width
    assert n == (3 + N_MID) * width + N_BRANCH * d and d % tn == 0
    tm = _tile(s, 512)
    rope = tables is not None
    per_batch = mods.shape[0] > 1
    in_specs = [
        pl.BlockSpec((1, tm, d), lambda bi, i, j: (bi, i, 0)),
        pl.BlockSpec((1, N_MOD, d), (lambda bi, i, j: (bi, 0, 0)) if per_batch else (lambda bi, i, j: (0, 0, 0))),
        pl.BlockSpec((1, d), lambda bi, i, j: (0, 0)),
        pl.BlockSpec((d, tn), lambda bi, i, j: (0, j)),
    ]
    args = [h, mods, g, w]
    if rope:
        in_specs += [pl.BlockSpec((tm, LANES), lambda bi, i, j: (i, 0))] * 3
        args += list(tables)
    return pl.pallas_call(
        functools.partial(_inproj_kernel, rope=rope, q_scale=q_scale),
        grid=(b, s // tm, n // tn),
        in_specs=in_specs,
        out_specs=[
            pl.BlockSpec((1, tm, tn), lambda bi, i, j: (bi, i, jnp.minimum(j, 2))),
            pl.BlockSpec((1, tm, tn), lambda bi, i, j: (bi, i, jnp.clip(j - 3, 0, N_MID - 1))),
            pl.BlockSpec((1, tm, tn), lambda bi, i, j: (bi, i, jnp.maximum(j - 3 - N_MID, 0))),
        ],
        out_shape=[
            jax.ShapeDtypeStruct((b, s, 3 * width), BF16),
            jax.ShapeDtypeStruct((b, s, N_MID * width), F32),
            jax.ShapeDtypeStruct((b, s, N_BRANCH * d), F32),
        ],
        scratch_shapes=[pltpu.VMEM((tm, d), BF16)],
        compiler_params=_params("arbitrary", "arbitrary", "arbitrary"),
        name="inproj_rope" if rope else "inproj",
    )(*args)


def _attn_kernel(*refs, heads, hd, has_lat):
    if has_lat:
        (q_ref, kc_ref, vc_ref, kl_ref, vl_ref, dl_ref, sg_ref, li_ref, o_ref,
         qs_ref, m_ref, l_ref, acc_ref) = refs
    else:
        q_ref, kc_ref, vc_ref, dl_ref, sg_ref, li_ref, o_ref, qs_ref, m_ref, l_ref, acc_ref = refs
    j = pl.program_id(2)
    last = pl.num_programs(2) - 1
    tq = q_ref.shape[1]
    hw = 2 * hd

    @pl.when(j == 0)
    def _():
        lane = lax.broadcasted_iota(jnp.int32, (tq, hw), 1)
        for h in range(heads):
            q = q_ref[0, :, h * hw:(h + 1) * hw].astype(F32)
            qs_ref[h, 0:tq] = jnp.where(lane < hd, q, 0.0).astype(BF16)
            qs_ref[h, tq:2 * tq] = jnp.where(lane >= hd, q, 0.0).astype(BF16)
        m_ref[...] = jnp.full(m_ref.shape, -jnp.inf, F32)
        l_ref[...] = jnp.zeros(l_ref.shape, F32)
        acc_ref[...] = jnp.zeros(acc_ref.shape, F32)

    def step(k_ref, v_ref):
        for h in range(heads):
            k = k_ref[0, :, h * hw:(h + 1) * hw]
            v = v_ref[0, :, h * hw:(h + 1) * hw]
            s = lax.dot_general(qs_ref[h], k, (((1,), (1,)), ((), ())), preferred_element_type=F32)
            m_prev = m_ref[h]
            m_new = jnp.maximum(m_prev, jnp.max(s, axis=-1, keepdims=True))
            alpha = jnp.exp(m_prev - m_new)
            p = jnp.exp(s - m_new)
            l_ref[h] = alpha * l_ref[h] + jnp.sum(p, axis=-1, keepdims=True)
            acc_ref[h] = alpha * acc_ref[h] + jnp.dot(p.astype(BF16), v, preferred_element_type=F32)
            m_ref[h] = m_new

    if has_lat:
        @pl.when(j == 0)
        def _():
            step(kc_ref, vc_ref)

        @pl.when(j > 0)
        def _():
            step(kl_ref, vl_ref)
    else:
        step(kc_ref, vc_ref)

    @pl.when(j == last)
    def _():
        dl = dl_ref[...]
        lam_init = li_ref[...]
        lam = (jnp.exp(jnp.sum(dl[0:1] * dl[1:2], axis=-1, keepdims=True))
               - jnp.exp(jnp.sum(dl[2:3] * dl[3:4], axis=-1, keepdims=True)) + lam_init)
        for h in range(heads):
            acc = acc_ref[h]
            l = l_ref[h]
            o = acc[0:tq] / l[0:tq] - lam * (acc[tq:2 * tq] / l[tq:2 * tq])
            o = _rms(o, sg_ref[...]) * (1.0 - lam_init)
            o_ref[0, :, h * hw:(h + 1) * hw] = o.astype(BF16)


def _attn(qkv_q, qkv_c, qkv_l, diff_lambda, subln, lam_init, *, hd):
    b, sq, w3 = qkv_q.shape
    width = w3 // 3
    heads = width // (2 * hd)
    n_ctx = qkv_c.shape[1]
    has_lat = qkv_l is not None
    tq = _tile(sq, 256)
    in_specs = [
        pl.BlockSpec((1, tq, width), lambda bi, i, j: (bi, i, 0)),
        pl.BlockSpec((1, n_ctx, width), lambda bi, i, j: (bi, 0, 1)),
        pl.BlockSpec((1, n_ctx, width), lambda bi, i, j: (bi, 0, 2)),
    ]
    args = [qkv_q, qkv_c, qkv_c]
    nkv = 1
    if has_lat:
        sk = qkv_l.shape[1]
        tk = _tile(sk, 1024)
        nkv += sk // tk
        in_specs += [
            pl.BlockSpec((1, tk, width), lambda bi, i, j: (bi, jnp.maximum(j - 1, 0), 1)),
            pl.BlockSpec((1, tk, width), lambda bi, i, j: (bi, jnp.maximum(j - 1, 0), 2)),
        ]
        args += [qkv_l, qkv_l]
    in_specs += [
        pl.BlockSpec(diff_lambda.shape, lambda bi, i, j: (0, 0)),
        pl.BlockSpec((1, 2 * hd), lambda bi, i, j: (0, 0)),
        pl.BlockSpec((1, 1), lambda bi, i, j: (0, 0)),
    ]
    args += [diff_lambda, subln, lam_init]
    return pl.pallas_call(
        functools.partial(_attn_kernel, heads=heads, hd=hd, has_lat=has_lat),
        grid=(b, sq // tq, nkv),
        in_specs=in_specs,
        out_specs=pl.BlockSpec((1, tq, width), lambda bi, i, j: (bi, i, 0)),
        out_shape=jax.ShapeDtypeStruct((b, sq, width), BF16),
        scratch_shapes=[
            pltpu.VMEM((heads, 2 * tq, 2 * hd), BF16),
            pltpu.VMEM((heads, 2 * tq, 1), F32),
            pltpu.VMEM((heads, 2 * tq, 1), F32),
            pltpu.VMEM((heads, 2 * tq, 2 * hd), F32),
        ],
        compiler_params=_params("arbitrary", "arbitrary", "arbitrary"),
        name="attn_lat" if has_lat else "attn_ctx",
    )(*args)


def _prep_kernel(x3_ref, x4_ref, x5_ref, x7_ref, cw_ref, cb_ref, rw_ref, rb_ref, yb_ref, xr_ref):
    s = x3_ref.shape[1]
    row = lax.broadcasted_iota(jnp.int32, (s, x3_ref.shape[2]), 0)

    def shifted(x, k):
        r = pltpu.roll(x, k % s, 0)
        return jnp.where((row >= k) & (row < s + k), r, 0.0)

    z = x4_ref[0] * x3_ref[0]
    cw = cw_ref[...]
    conv = cw[0:1] * shifted(z, 1) + cw[1:2] * z + cw[2:3] * shifted(z, -1) + cb_ref[...]
    yb_ref[0] = (x5_ref[0] * conv).astype(BF16)
    x = x7_ref[0]
    rw = rw_ref[...]
    xr_ref[0] = (rw[0:1] * shifted(x, 2) + rw[1:2] * shifted(x, 1) + rw[2:3] * x + rw[3:4] * shifted(x, -1)
                 + rb_ref[...])


def _prep(rest, conv_w, conv_b, rnn_conv_w, rnn_conv_b, *, conv_width, rnn_width):
    b, s, _ = rest.shape
    assert conv_width == rnn_width
    tc = LANES
    nct = conv_width // tc
    col = lambda k: (lambda bi, c: (bi, 0, k * nct + c))
    par = lambda rows: pl.BlockSpec((rows, tc), lambda bi, c: (0, c))
    return pl.pallas_call(
        _prep_kernel,
        grid=(b, nct),
        in_specs=[pl.BlockSpec((1, s, tc), col(0)), pl.BlockSpec((1, s, tc), col(1)),
                  pl.BlockSpec((1, s, tc), col(2)), pl.BlockSpec((1, s, tc), col(4)),
                  par(conv_w.shape[0]), par(1), par(rnn_conv_w.shape[0]), par(1)],
        out_specs=[pl.BlockSpec((1, s, tc), lambda bi, c: (bi, 0, c)),
                   pl.BlockSpec((1, s, tc), lambda bi, c: (bi, 0, c))],
        out_shape=[jax.ShapeDtypeStruct((b, s, conv_width), BF16),
                   jax.ShapeDtypeStruct((b, s, rnn_width), F32)],
        compiler_params=_params("arbitrary", "arbitrary"),
        name="conv_prep",
    )(rest, rest, rest, rest, conv_w, conv_b, rnn_conv_w, rnn_conv_b)


def _scan_kernel(*refs, reverse, finalize):
    if finalize:
        (xr_ref, wa_ref, wx_ref, ba_ref, bx_ref, lam_ref, h0_ref, hf_ref, gate_ref, out_ref, hlast_ref,
         a_s, g_s, h_s, carry) = refs
    else:
        xr_ref, wa_ref, wx_ref, ba_ref, bx_ref, lam_ref, h0_ref, out_ref, hlast_ref, a_s, g_s, h_s, carry = refs
    i = pl.program_id(1)
    nb, tc, cw = xr_ref.shape

    @pl.when(i == 0)
    def _():
        carry[...] = h0_ref[...]

    x = xr_ref[...].reshape(nb * tc, cw)
    xb = x.astype(BF16)
    r = jax.nn.sigmoid(jnp.dot(xb, wa_ref[0, 0].astype(BF16), preferred_element_type=F32) + ba_ref[0, 0])
    gi = jax.nn.sigmoid(jnp.dot(xb, wx_ref[0, 0].astype(BF16), preferred_element_type=F32) + bx_ref[0, 0])
    z = -lam_ref[0, 0]
    softplus = jnp.maximum(z, 0.0) + jnp.log1p(jnp.exp(-jnp.abs(z)))
    a = jnp.exp(-RG_C * r * softplus)
    a_s[...] = a
    g_s[...] = jnp.sqrt(1.0 - a * a) * (gi * x)

    def body(t, h):
        tt = tc - 1 - t if reverse else t
        rows = pl.ds(tt, nb, stride=tc)
        h = a_s[rows, :] * h + g_s[rows, :]
        h_s[rows, :] = h
        return h

    h = lax.fori_loop(0, tc, body, carry[...], unroll=8)
    carry[...] = h
    hlast_ref[...] = h
    hs = h_s[...].reshape(nb, tc, cw)
    if finalize:
        out_ref[...] = (jax.nn.gelu(gate_ref[...]) * (hf_ref[...] + hs)).astype(out_ref.dtype)
    else:
        out_ref[...] = hs


def _scan(xr, wa, wx, ba, bx, lam, h0, hf, rest, *, direction, gate_col):
    b, s, c = xr.shape
    nblk, bw = wa.shape[1], wa.shape[2]
    assert bw == LANES and nblk * bw == c
    tc = _tile(s, 512)
    nchunk = s // tc
    reverse = direction == 1
    finalize = hf is not None
    chunk = (lambda i: nchunk - 1 - i) if reverse else (lambda i: i)
    d = direction
    seq_spec = pl.BlockSpec((b, tc, bw), lambda n, i: (0, chunk(i), n))
    w_spec = pl.BlockSpec((1, 1, bw, bw), lambda n, i: (d, n, 0, 0))
    v_spec = pl.BlockSpec((1, 1, 1, bw), lambda n, i: (d, n, 0, 0))
    st_spec = pl.BlockSpec((b, bw), lambda n, i: (0, n))
    in_specs = [seq_spec, w_spec, w_spec, v_spec, v_spec, v_spec, st_spec]
    args = [xr, wa, wx, ba, bx, lam, h0]
    if finalize:
        in_specs += [seq_spec, pl.BlockSpec((b, tc, bw), lambda n, i: (0, chunk(i), gate_col * nblk + n))]
        args += [hf, rest]
    return pl.pallas_call(
        functools.partial(_scan_kernel, reverse=reverse, finalize=finalize),
        grid=(nblk, nchunk),
        in_specs=in_specs,
        out_specs=[seq_spec, st_spec],
        out_shape=[jax.ShapeDtypeStruct((b, s, c), BF16 if finalize else F32),
                   jax.ShapeDtypeStruct((b, c), F32)],
        scratch_shapes=[pltpu.VMEM((b * tc, bw), F32)] * 3 + [pltpu.VMEM((b, bw), F32)],
        compiler_params=_params("arbitrary", "arbitrary"),
        name="rglru_bwd" if reverse else "rglru_fwd",
    )(*args)


def _merge_kernel(ya_ref, yb_ref, yc_ref, g0_ref, g1_ref, g2_ref, bm_ref, wa_ref, wb_ref, wc_ref, wo_ref,
                  h_ref, mod_ref, gp_ref, o_ref):
    bm = bm_ref[...]
    m = jax.nn.sigmoid(g0_ref[0] + bm[0:1]) * jnp.dot(ya_ref[0], wa_ref[...], preferred_element_type=F32)
    m += jax.nn.sigmoid(g1_ref[0] + bm[1:2]) * jnp.dot(yb_ref[0], wb_ref[...], preferred_element_type=F32)
    m += jax.nn.sigmoid(g2_ref[0] + bm[2:3]) * jnp.dot(yc_ref[0], wc_ref[...], preferred_element_type=F32)
    out = jnp.dot(m.astype(BF16), wo_ref[...], preferred_element_type=F32)
    o_ref[0] = h_ref[0] + mod_ref[0][2:3] * _rms(out, gp_ref[...])


def _merge(ya, yb, yc, gates, b_merge, wba, wbb, wbc, wo, h, mods, g_post):
    b, s, d = h.shape
    tm = _tile(s, 256)
    per_batch = mods.shape[0] > 1
    row = lambda w: pl.BlockSpec((1, tm, w), lambda bi, i: (bi, i, 0))
    gate = lambda k: pl.BlockSpec((1, tm, d), lambda bi, i: (bi, i, k))
    full = lambda a: pl.BlockSpec(a.shape, lambda bi, i: (0, 0))
    return pl.pallas_call(
        _merge_kernel,
        grid=(b, s // tm),
        in_specs=[row(ya.shape[2]), row(yb.shape[2]), row(yc.shape[2]), gate(0), gate(1), gate(2),
                  full(b_merge), full(wba), full(wbb), full(wbc), full(wo), row(d),
                  pl.BlockSpec((1, N_MOD, d), (lambda bi, i: (bi, 0, 0)) if per_batch else (lambda bi, i: (0, 0, 0))),
                  pl.BlockSpec((1, d), lambda bi, i: (0, 0))],
        out_specs=row(d),
        out_shape=jax.ShapeDtypeStruct((b, s, d), F32),
        compiler_params=_params("arbitrary", "arbitrary"),
        name="merge",
    )(ya, yb, yc, gates, gates, gates, b_merge, wba, wbb, wbc, wo, h, mods, g_post)


def _ffn_kernel(h_ref, mod_ref, gpre_ref, gpost_ref, wg_ref, wu_ref, wd_ref, o_ref, un_ref, acc_ref):
    j = pl.program_id(2)

    @pl.when(j == 0)
    def _():
        m = mod_ref[0]
        un_ref[...] = (_rms(h_ref[0], gpre_ref[...]) * (1.0 + m[4:5]) + m[3:4]).astype(BF16)
        acc_ref[...] = jnp.zeros(acc_ref.shape, F32)

    u = un_ref[...]
    hg = jnp.dot(u, wg_ref[...], preferred_element_type=F32)
    hu = jnp.dot(u, wu_ref[...], preferred_element_type=F32)
    acc_ref[...] += jnp.dot((_silu(hg) * hu).astype(BF16), wd_ref[...], preferred_element_type=F32)

    @pl.when(j == pl.num_programs(2) - 1)
    def _():
        o_ref[0] = h_ref[0] + mod_ref[0][5:6] * _rms(acc_ref[...], gpost_ref[...])


def _ffn(h, mods, g_pre, g_post, wg, wu, wd):
    b, s, d = h.shape
    hidden = wg.shape[1]
    tm = _tile(s, 512)
    th = _tile(hidden, 512)
    per_batch = mods.shape[0] > 1
    return pl.pallas_call(
        _ffn_kernel,
        grid=(b, s // tm, hidden // th),
        in_specs=[pl.BlockSpec((1, tm, d), lambda bi, i, j: (bi, i, 0)),
                  pl.BlockSpec((1, N_MOD, d),
                               (lambda bi, i, j: (bi, 0, 0)) if per_batch else (lambda bi, i, j: (0, 0, 0))),
                  pl.BlockSpec((1, d), lambda bi, i, j: (0, 0)),
                  pl.BlockSpec((1, d), lambda bi, i, j: (0, 0)),
                  pl.BlockSpec((d, th), lambda bi, i, j: (0, j)),
                  pl.BlockSpec((d, th), lambda bi, i, j: (0, j)),
                  pl.BlockSpec((th, d), lambda bi, i, j: (j, 0))],
        out_specs=pl.BlockSpec((1, tm, d), lambda bi, i, j: (bi, i, 0)),
        out_shape=jax.ShapeDtypeStruct((b, s, d), F32),
        scratch_shapes=[pltpu.VMEM((tm, d), BF16), pltpu.VMEM((tm, d), F32)],
        compiler_params=_params("arbitrary", "arbitrary", "arbitrary"),
        name="ffn",
    )(h, mods, g_pre, g_post, wg, wu, wd)


def _rope_tables(n_tokens, hd):
    freqs = hd // 4
    pos = jnp.arange(n_tokens)
    rowcol = jnp.stack([(pos // GRID_W).astype(F32), (pos % GRID_W).astype(F32)], axis=1)
    inv = ROPE_BASE ** (-jnp.arange(freqs, dtype=F32) / freqs)
    lane = jnp.arange(LANES) % hd
    axis, half, f = lane // (2 * freqs), (lane % (2 * freqs)) // freqs, lane % freqs
    ang = rowcol[:, axis] * inv[f][None, :]
    cos, sin = jnp.cos(ang), jnp.sin(ang)
    return cos, jnp.where(half == 1, sin, 0.0), jnp.where(half == 0, -sin, 0.0)


def kernel(x, c, ctx, c_ctx, w_ada, b_ada, g_pre_mix, g_post_mix, g_pre_ffn, g_post_ffn, w_in, diff_lambda, diff_subln, conv_w, conv_b, rnn_conv_w, rnn_conv_b, rg_wa, rg_ba, rg_wx, rg_bx, rg_lambda, b_merge, w_branch_a, w_branch_b, w_branch_c, w_o, w_ffn_gate, w_ffn_up, w_ffn_down):
    b, s, d = x.shape
    depth = w_ada.shape[0]
    hd = diff_lambda.shape[-1]
    diff_w = w_branch_a.shape[1]
    conv_width = conv_w.shape[-1]
    rnn_width = rnn_conv_w.shape[-1]
    nblk, bw = rg_wa.shape[2], rg_wa.shape[3]
    assert diff_w == conv_width == rnn_width and 2 * hd == LANES
    gate_col_rnn = 3

    rows = -(-(b + 1) // SUBLANES) * SUBLANES
    cc = jnp.zeros((rows, d), F32).at[:b].set(c).at[b].set(c_ctx)
    mods = _ada(cc, w_ada, b_ada.reshape(depth, 1, N_MOD * d))
    tables = _rope_tables(s, hd)
    q_scale = hd ** -0.5

    vec = lambda a: a.reshape(2, nblk, 1, bw)
    h_lat, h_ctx = x, ctx
    for l in range(depth):
        need_ctx = l < depth - 1
        lam_init = 0.8 - 0.6 * math.exp(-0.3 * l)
        li = jnp.full((1, 1), lam_init, F32)
        ml = mods[l, :b].reshape(b, N_MOD, d)
        mc = mods[l, b:b + 1].reshape(1, N_MOD, d)
        w_in_l = w_in[l].astype(BF16)
        g_pre = g_pre_mix[l].reshape(1, d)
        subln = diff_subln[l].reshape(1, 2 * hd)

        qkv_l, rest_l, gates_l = _inproj(h_lat, ml, g_pre, w_in_l, tables, width=diff_w, q_scale=q_scale)
        qkv_c, rest_c, gates_c = _inproj(h_ctx, mc, g_pre, w_in_l, None, width=diff_w, q_scale=q_scale)

        ya_l = _attn(qkv_l, qkv_c, qkv_l, diff_lambda[l], subln, li, hd=hd)

        prep = functools.partial(_prep, conv_w=conv_w[l], conv_b=conv_b[l].reshape(1, -1),
                                 rnn_conv_w=rnn_conv_w[l], rnn_conv_b=rnn_conv_b[l].reshape(1, -1),
                                 conv_width=conv_width, rnn_width=rnn_width)
        yb_l, xr_l = prep(rest_l)
        yb_c, xr_c = prep(rest_c)

        scan = functools.partial(_scan, wa=rg_wa[l], wx=rg_wx[l], ba=vec(rg_ba[l]), bx=vec(rg_bx[l]),
                                 lam=vec(rg_lambda[l]), gate_col=gate_col_rnn)
        zero = jnp.zeros((b, rnn_width), F32)
        hf_c, hfin_f = scan(xr_c, h0=zero, hf=None, rest=None, direction=0)
        yc_c, hfin_b = scan(xr_c, h0=zero, hf=hf_c, rest=rest_c, direction=1)
        hf_l, _ = scan(xr_l, h0=hfin_f, hf=None, rest=None, direction=0)
        yc_l, _ = scan(xr_l, h0=hfin_b, hf=hf_l, rest=rest_l, direction=1)

        wba, wbb, wbc = w_branch_a[l].astype(BF16), w_branch_b[l].astype(BF16), w_branch_c[l].astype(BF16)
        wo = w_o[l].astype(BF16)
        wg, wu, wd = w_ffn_gate[l].astype(BF16), w_ffn_up[l].astype(BF16), w_ffn_down[l].astype(BF16)
        g_post = g_post_mix[l].reshape(1, d)
        gf_pre, gf_post = g_pre_ffn[l].reshape(1, d), g_post_ffn[l].reshape(1, d)

        h_lat = _merge(ya_l, yb_l, yc_l, gates_l, b_merge[l], wba, wbb, wbc, wo, h_lat, ml, g_post)
        h_lat = _ffn(h_lat, ml, gf_pre, gf_post, wg, wu, wd)
        if need_ctx:
            ya_c = _attn(qkv_c, qkv_c, None, diff_lambda[l], subln, li, hd=hd)
            h_ctx = _merge(ya_c, yb_c, yc_c, gates_c, b_merge[l], wba, wbb, wbc, wo, h_ctx, mc, g_post)
            h_ctx = _ffn(h_ctx, mc, gf_pre, gf_post, wg, wu, wd)
    return h_lat
```

```python
import functools
import math

import jax
import jax.numpy as jnp
from jax import lax
from jax.experimental import pallas as pl
from jax.experimental.pallas import tpu as pltpu

GRID_W = 64
ROPE_BASE = 10000.0
EPS = 1e-6
RG_C = 8.0
N_MOD = 6
N_BRANCH = 3
N_MID = 5
LANES = 128
SUBLANES = 8
VMEM_LIMIT_BYTES = 52 * 1024 * 1024

F32 = jnp.float32
BF16 = jnp.bfloat16


def _tile(n, pref):
    if n <= pref:
        return n
    t = pref - pref % SUBLANES
    while t >= SUBLANES:
        if n % t == 0:
            return t
        t -= SUBLANES
    return n


def _params(*sem):
    return pltpu.CompilerParams(dimension_semantics=sem, vmem_limit_bytes=VMEM_LIMIT_BYTES)


def _rms(x, g):
    return x * lax.rsqrt(jnp.mean(x * x, axis=-1, keepdims=True) + EPS) * g


def _silu(x):
    return x * jax.nn.sigmoid(x)


def _ada_kernel(x_ref, w_ref, b_ref, o_ref):
    sx = _silu(x_ref[...]).astype(BF16)
    o_ref[0] = jnp.dot(sx, w_ref[0].astype(BF16), preferred_element_type=F32) + b_ref[0]


def _ada(cc, w_ada, b_flat):
    depth, d, n = w_ada.shape
    rows = cc.shape[0]
    tn = _tile(n, 1024)
    return pl.pallas_call(
        _ada_kernel,
        grid=(depth, n // tn),
        in_specs=[
            pl.BlockSpec((rows, d), lambda l, j: (0, 0)),
            pl.BlockSpec((1, d, tn), lambda l, j: (l, 0, j)),
            pl.BlockSpec((1, 1, tn), lambda l, j: (l, 0, j)),
        ],
        out_specs=pl.BlockSpec((1, rows, tn), lambda l, j: (l, 0, j)),
        out_shape=jax.ShapeDtypeStruct((depth, rows, n), F32),
        compiler_params=_params("arbitrary", "arbitrary"),
        name="ada",
    )(cc, w_ada, b_flat)


def _inproj_kernel(h_ref, mod_ref, g_ref, w_ref, *rest, rope, q_scale):
    if rope:
        cos_ref, s1_ref, s2_ref, qkv_ref, vt_ref, mid_ref, gates_ref, xn_ref = rest
    else:
        qkv_ref, vt_ref, mid_ref, gates_ref, xn_ref = rest
    j = pl.program_id(2)

    @pl.when(j == 0)
    def _():
        m = mod_ref[0]
        y = _rms(h_ref[0], g_ref[...])
        xn_ref[...] = (y * (1.0 + m[1:2]) + m[0:1]).astype(BF16)

    acc = jnp.dot(xn_ref[...], w_ref[...], preferred_element_type=F32)
    tn = acc.shape[1]

    @pl.when(j < 2)
    def _():
        a = acc * jnp.where(j == 0, q_scale, 1.0)
        if rope:
            cos, s1, s2 = cos_ref[...], s1_ref[...], s2_ref[...]
            quarter = LANES // 8
            for c in range(tn // LANES):
                blk = a[:, c * LANES:(c + 1) * LANES]
                r = blk * cos + pltpu.roll(blk, quarter, 1) * s1 + pltpu.roll(blk, LANES - quarter, 1) * s2
                qkv_ref[0, :, c * LANES:(c + 1) * LANES] = r.astype(BF16)
        else:
            qkv_ref[0] = a.astype(BF16)

    @pl.when(j == 2)
    def _():
        vt_ref[0] = acc.T.astype(BF16)

    @pl.when((j >= 3) & (j < 3 + N_MID))
    def _():
        mid_ref[0] = acc

    @pl.when(j >= 3 + N_MID)
    def _():
        gates_ref[0] = acc


def _inproj(h, mods, g, w, tables, *, width, q_scale):
    b, s, d = h.shape
    n = w.shape[1]
    tn = width
    assert n == (3 + N_MID) * width + N_BRANCH * d and d % tn == 0
    tm = _tile(s, 512)
    rope = tables is not None
    per_batch = mods.shape[0] > 1
    in_specs = [
        pl.BlockSpec((1, tm, d), lambda bi, i, j: (bi, i, 0)),
        pl.BlockSpec((1, N_MOD, d), (lambda bi, i, j: (bi, 0, 0)) if per_batch else (lambda bi, i, j: (0, 0, 0))),
        pl.BlockSpec((1, d), lambda bi, i, j: (0, 0)),
        pl.BlockSpec((d, tn), lambda bi, i, j: (0, j)),
    ]
    args = [h, mods, g, w]
    if rope:
        in_specs += [pl.BlockSpec((tm, LANES), lambda bi, i, j: (i, 0))] * 3
        args += list(tables)
    return pl.pallas_call(
        functools.partial(_inproj_kernel, rope=rope, q_scale=q_scale),
        grid=(b, s // tm, n // tn),
        in_specs=in_specs,
        out_specs=[
            pl.BlockSpec((1, tm, tn), lambda bi, i, j: (bi, i, jnp.minimum(j, 1))),
            pl.BlockSpec((1, tn, tm), lambda bi, i, j: (bi, 0, i)),
            pl.BlockSpec((1, tm, tn), lambda bi, i, j: (bi, i, jnp.clip(j - 3, 0, N_MID - 1))),
            pl.BlockSpec((1, tm, tn), lambda bi, i, j: (bi, i, jnp.maximum(j - 3 - N_MID, 0))),
        ],
        out_shape=[
            jax.ShapeDtypeStruct((b, s, 2 * width), BF16),
            jax.ShapeDtypeStruct((b, width, s), BF16),
            jax.ShapeDtypeStruct((b, s, N_MID * width), F32),
            jax.ShapeDtypeStruct((b, s, N_BRANCH * d), F32),
        ],
        scratch_shapes=[pltpu.VMEM((tm, d), BF16)],
        compiler_params=_params("arbitrary", "arbitrary", "arbitrary"),
        name="inproj_rope" if rope else "inproj",
    )(*args)


def _attn_kernel(*refs, heads, hd, has_lat):
    if has_lat:
        (q_ref, kc_ref, vc_ref, kl_ref, vl_ref, dl_ref, sg_ref, li_ref, o_ref,
         qs_ref, m_ref, l_ref, acc_ref, s_ref) = refs
    else:
        q_ref, kc_ref, vc_ref, dl_ref, sg_ref, li_ref, o_ref, qs_ref, m_ref, l_ref, acc_ref, s_ref = refs
    j = pl.program_id(2)
    last = pl.num_programs(2) - 1
    tq = q_ref.shape[1]
    hw = 2 * hd

    @pl.when(j == 0)
    def _():
        lane = lax.broadcasted_iota(jnp.int32, (tq, hw), 1)
        for h in range(heads):
            q = q_ref[0, :, h * hw:(h + 1) * hw].astype(F32)
            qs_ref[h, 0:tq] = jnp.where(lane < hd, q, 0.0).astype(BF16)
            qs_ref[h, tq:2 * tq] = jnp.where(lane >= hd, q, 0.0).astype(BF16)
        m_ref[...] = jnp.full(m_ref.shape, -jnp.inf, F32)
        l_ref[...] = jnp.zeros(l_ref.shape, F32)
        acc_ref[...] = jnp.zeros(acc_ref.shape, F32)

    def step(k_ref, vt_ref):
        tk = k_ref.shape[1]

        def scores(h):
            k = k_ref[0, :, h * hw:(h + 1) * hw]
            s_ref[h % 2, 0:tk, :] = lax.dot_general(k, qs_ref[h], (((1,), (1,)), ((), ())),
                                                    preferred_element_type=F32)

        scores(0)
        for h in range(heads):
            if h + 1 < heads:
                scores(h + 1)
            vt = vt_ref[0, h * hw:(h + 1) * hw, :]
            s = s_ref[h % 2, 0:tk, :]
            m_prev = m_ref[h]
            m_new = jnp.maximum(m_prev, jnp.max(s, axis=0, keepdims=True))
            alpha = jnp.exp2(m_prev - m_new)
            p = jnp.exp2(s - m_new)
            l_ref[h] = alpha * l_ref[h] + jnp.sum(p, axis=0, keepdims=True)
            acc_ref[h] = alpha * acc_ref[h] + jnp.dot(vt, p.astype(BF16), preferred_element_type=F32)
            m_ref[h] = m_new

    if has_lat:
        @pl.when(j == 0)
        def _():
            step(kc_ref, vc_ref)

        @pl.when(j > 0)
        def _():
            step(kl_ref, vl_ref)
    else:
        step(kc_ref, vc_ref)

    @pl.when(j == last)
    def _():
        dl = dl_ref[...]
        lam_init = li_ref[...]
        lam = (jnp.exp(jnp.sum(dl[0:1] * dl[1:2], axis=-1, keepdims=True))
               - jnp.exp(jnp.sum(dl[2:3] * dl[3:4], axis=-1, keepdims=True)) + lam_init)
        for h in range(heads):
            acc = acc_ref[h]
            l = l_ref[h]
            o = acc[:, 0:tq] / l[:, 0:tq] - lam * (acc[:, tq:2 * tq] / l[:, tq:2 * tq])
            o = o * lax.rsqrt(jnp.mean(o * o, axis=0, keepdims=True) + EPS) * sg_ref[...] * (1.0 - lam_init)
            o_ref[0, :, h * hw:(h + 1) * hw] = o.T.astype(BF16)


def _attn(qk_q, qk_c, vt_c, qk_l, vt_l, diff_lambda, subln, lam_init, *, hd):
    b, sq, w2 = qk_q.shape
    width = w2 // 2
    heads = width // (2 * hd)
    n_ctx = qk_c.shape[1]
    has_lat = qk_l is not None
    tq = _tile(sq, 256)
    in_specs = [
        pl.BlockSpec((1, tq, width), lambda bi, i, j: (bi, i, 0)),
        pl.BlockSpec((1, n_ctx, width), lambda bi, i, j: (bi, 0, 1)),
        pl.BlockSpec((1, width, n_ctx), lambda bi, i, j: (bi, 0, 0)),
    ]
    args = [qk_q, qk_c, vt_c]
    nkv = 1
    tk = n_ctx
    if has_lat:
        sk = qk_l.shape[1]
        tk = _tile(sk, 1024)
        assert tk >= n_ctx
        nkv += sk // tk
        in_specs += [
            pl.BlockSpec((1, tk, width), lambda bi, i, j: (bi, jnp.maximum(j - 1, 0), 1)),
            pl.BlockSpec((1, width, tk), lambda bi, i, j: (bi, 0, jnp.maximum(j - 1, 0))),
        ]
        args += [qk_l, vt_l]
    in_specs += [
        pl.BlockSpec(diff_lambda.shape, lambda bi, i, j: (0, 0)),
        pl.BlockSpec((2 * hd, 1), lambda bi, i, j: (0, 0)),
        pl.BlockSpec((1, 1), lambda bi, i, j: (0, 0)),
    ]
    args += [diff_lambda, subln, lam_init]
    return pl.pallas_call(
        functools.partial(_attn_kernel, heads=heads, hd=hd, has_lat=has_lat),
        grid=(b, sq // tq, nkv),
        in_specs=in_specs,
        out_specs=pl.BlockSpec((1, tq, width), lambda bi, i, j: (bi, i, 0)),
        out_shape=jax.ShapeDtypeStruct((b, sq, width), BF16),
        scratch_shapes=[
            pltpu.VMEM((heads, 2 * tq, 2 * hd), BF16),
            pltpu.VMEM((heads, 1, 2 * tq), F32),
            pltpu.VMEM((heads, 1, 2 * tq), F32),
            pltpu.VMEM((heads, 2 * hd, 2 * tq), F32),
            pltpu.VMEM((2, tk, 2 * tq), F32),
        ],
        compiler_params=_params("arbitrary", "arbitrary", "arbitrary"),
        name="attn_lat" if has_lat else "attn_ctx",
    )(*args)


def _prep_kernel(x3_ref, x4_ref, x5_ref, x7_ref, cw_ref, cb_ref, rw_ref, rb_ref, yb_ref, xr_ref):
    s = x3_ref.shape[1]
    row = lax.broadcasted_iota(jnp.int32, (s, x3_ref.shape[2]), 0)

    def shifted(x, k):
        r = pltpu.roll(x, k % s, 0)
        return jnp.where((row >= k) & (row < s + k), r, 0.0)

    z = x4_ref[0] * x3_ref[0]
    cw = cw_ref[...]
    conv = cw[0:1] * shifted(z, 1) + cw[1:2] * z + cw[2:3] * shifted(z, -1) + cb_ref[...]
    yb_ref[0] = (x5_ref[0] * conv).astype(BF16)
    x = x7_ref[0]
    rw = rw_ref[...]
    xr_ref[0] = (rw[0:1] * shifted(x, 2) + rw[1:2] * shifted(x, 1) + rw[2:3] * x + rw[3:4] * shifted(x, -1)
                 + rb_ref[...])


def _prep(rest, conv_w, conv_b, rnn_conv_w, rnn_conv_b, *, conv_width, rnn_width):
    b, s, _ = rest.shape
    assert conv_width == rnn_width
    tc = LANES
    nct = conv_width // tc
    col = lambda k: (lambda bi, c: (bi, 0, k * nct + c))
    par = lambda rows: pl.BlockSpec((rows, tc), lambda bi, c: (0, c))
    return pl.pallas_call(
        _prep_kernel,
        grid=(b, nct),
        in_specs=[pl.BlockSpec((1, s, tc), col(0)), pl.BlockSpec((1, s, tc), col(1)),
                  pl.BlockSpec((1, s, tc), col(2)), pl.BlockSpec((1, s, tc), col(4)),
                  par(conv_w.shape[0]), par(1), par(rnn_conv_w.shape[0]), par(1)],
        out_specs=[pl.BlockSpec((1, s, tc), lambda bi, c: (bi, 0, c)),
                   pl.BlockSpec((1, s, tc), lambda bi, c: (bi, 0, c))],
        out_shape=[jax.ShapeDtypeStruct((b, s, conv_width), BF16),
                   jax.ShapeDtypeStruct((b, s, rnn_width), F32)],
        compiler_params=_params("arbitrary", "arbitrary"),
        name="conv_prep",
    )(rest, rest, rest, rest, conv_w, conv_b, rnn_conv_w, rnn_conv_b)


def _scan_kernel(*refs, reverse, finalize):
    if finalize:
        (xr_ref, wa_ref, wx_ref, ba_ref, bx_ref, lam_ref, h0_ref, hf_ref, gate_ref, out_ref, hlast_ref,
         a_s, g_s, h_s, carry) = refs
    else:
        xr_ref, wa_ref, wx_ref, ba_ref, bx_ref, lam_ref, h0_ref, out_ref, hlast_ref, a_s, g_s, h_s, carry = refs
    i = pl.program_id(1)
    nb, tc, cw = xr_ref.shape

    @pl.when(i == 0)
    def _():
        carry[...] = h0_ref[...]

    x = xr_ref[...].reshape(nb * tc, cw)
    xb = x.astype(BF16)
    r = jax.nn.sigmoid(jnp.dot(xb, wa_ref[0, 0].astype(BF16), preferred_element_type=F32) + ba_ref[0, 0])
    gi = jax.nn.sigmoid(jnp.dot(xb, wx_ref[0, 0].astype(BF16), preferred_element_type=F32) + bx_ref[0, 0])
    z = -lam_ref[0, 0]
    softplus = jnp.maximum(z, 0.0) + jnp.log1p(jnp.exp(-jnp.abs(z)))
    a = jnp.exp(-RG_C * r * softplus)
    a_s[...] = a
    g_s[...] = jnp.sqrt(1.0 - a * a) * (gi * x)

    def body(t, h):
        tt = tc - 1 - t if reverse else t
        rows = pl.ds(tt, nb, stride=tc)
        h = a_s[rows, :] * h + g_s[rows, :]
        h_s[rows, :] = h
        return h

    h = lax.fori_loop(0, tc, body, carry[...], unroll=8)
    carry[...] = h
    hlast_ref[...] = h
    hs = h_s[...].reshape(nb, tc, cw)
    if finalize:
        out_ref[...] = (jax.nn.gelu(gate_ref[...]) * (hf_ref[...] + hs)).astype(out_ref.dtype)
    else:
        out_ref[...] = hs


def _scan(xr, wa, wx, ba, bx, lam, h0, hf, rest, *, direction, gate_col):
    b, s, c = xr.shape
    nblk, bw = wa.shape[1], wa.shape[2]
    assert bw == LANES and nblk * bw == c
    tc = _tile(s, 512)
    nchunk = s // tc
    reverse = direction == 1
    finalize = hf is not None
    chunk = (lambda i: nchunk - 1 - i) if reverse else (lambda i: i)
    d = direction
    seq_spec = pl.BlockSpec((b, tc, bw), lambda n, i: (0, chunk(i), n))
    w_spec = pl.BlockSpec((1, 1, bw, bw), lambda n, i: (d, n, 0, 0))
    v_spec = pl.BlockSpec((1, 1, 1, bw), lambda n, i: (d, n, 0, 0))
    st_spec = pl.BlockSpec((b, bw), lambda n, i: (0, n))
    in_specs = [seq_spec, w_spec, w_spec, v_spec, v_spec, v_spec, st_spec]
    args = [xr, wa, wx, ba, bx, lam, h0]
    if finalize:
        in_specs += [seq_spec, pl.BlockSpec((b, tc, bw), lambda n, i: (0, chunk(i), gate_col * nblk + n))]
        args += [hf, rest]
    return pl.pallas_call(
        functools.partial(_scan_kernel, reverse=reverse, finalize=finalize),
        grid=(nblk, nchunk),
        in_specs=in_specs,
        out_specs=[seq_spec, st_spec],
        out_shape=[jax.ShapeDtypeStruct((b, s, c), BF16 if finalize else F32),
                   jax.ShapeDtypeStruct((b, c), F32)],
        scratch_shapes=[pltpu.VMEM((b * tc, bw), F32)] * 3 + [pltpu.VMEM((b, bw), F32)],
        compiler_params=_params("arbitrary", "arbitrary"),
        name="rglru_bwd" if reverse else "rglru_fwd",
    )(*args)


def _merge_kernel(ya_ref, yb_ref, yc_ref, g0_ref, g1_ref, g2_ref, bm_ref, wa_ref, wb_ref, wc_ref, wo_ref,
                  h_ref, mod_ref, gp_ref, o_ref):
    bm = bm_ref[...]
    m = jax.nn.sigmoid(g0_ref[0] + bm[0:1]) * jnp.dot(ya_ref[0], wa_ref[...], preferred_element_type=F32)
    m += jax.nn.sigmoid(g1_ref[0] + bm[1:2]) * jnp.dot(yb_ref[0], wb_ref[...], preferred_element_type=F32)
    m += jax.nn.sigmoid(g2_ref[0] + bm[2:3]) * jnp.dot(yc_ref[0], wc_ref[...], preferred_element_type=F32)
    out = jnp.dot(m.astype(BF16), wo_ref[...], preferred_element_type=F32)
    o_ref[0] = h_ref[0] + mod_ref[0][2:3] * _rms(out, gp_ref[...])


def _merge(ya, yb, yc, gates, b_merge, wba, wbb, wbc, wo, h, mods, g_post):
    b, s, d = h.shape
    tm = _tile(s, 256)
    per_batch = mods.shape[0] > 1
    row = lambda w: pl.BlockSpec((1, tm, w), lambda bi, i: (bi, i, 0))
    gate = lambda k: pl.BlockSpec((1, tm, d), lambda bi, i: (bi, i, k))
    full = lambda a: pl.BlockSpec(a.shape, lambda bi, i: (0, 0))
    return pl.pallas_call(
        _merge_kernel,
        grid=(b, s // tm),
        in_specs=[row(ya.shape[2]), row(yb.shape[2]), row(yc.shape[2]), gate(0), gate(1), gate(2),
                  full(b_merge), full(wba), full(wbb), full(wbc), full(wo), row(d),
                  pl.BlockSpec((1, N_MOD, d), (lambda bi, i: (bi, 0, 0)) if per_batch else (lambda bi, i: (0, 0, 0))),
                  pl.BlockSpec((1, d), lambda bi, i: (0, 0))],
        out_specs=row(d),
        out_shape=jax.ShapeDtypeStruct((b, s, d), F32),
        compiler_params=_params("arbitrary", "arbitrary"),
        name="merge",
    )(ya, yb, yc, gates, gates, gates, b_merge, wba, wbb, wbc, wo, h, mods, g_post)


def _ffn_kernel(h_ref, mod_ref, gpre_ref, gpost_ref, wg_ref, wu_ref, wd_ref, o_ref, un_ref, acc_ref):
    j = pl.program_id(2)

    @pl.when(j == 0)
    def _():
        m = mod_ref[0]
        un_ref[...] = (_rms(h_ref[0], gpre_ref[...]) * (1.0 + m[4:5]) + m[3:4]).astype(BF16)
        acc_ref[...] = jnp.zeros(acc_ref.shape, F32)

    u = un_ref[...]
    hg = jnp.dot(u, wg_ref[...], preferred_element_type=F32)
    hu = jnp.dot(u, wu_ref[...], preferred_element_type=F32)
    acc_ref[...] += jnp.dot((_silu(hg) * hu).astype(BF16), wd_ref[...], preferred_element_type=F32)

    @pl.when(j == pl.num_programs(2) - 1)
    def _():
        o_ref[0] = h_ref[0] + mod_ref[0][5:6] * _rms(acc_ref[...], gpost_ref[...])


def _ffn(h, mods, g_pre, g_post, wg, wu, wd):
    b, s, d = h.shape
    hidden = wg.shape[1]
    tm = _tile(s, 512)
    th = _tile(hidden, 512)
    per_batch = mods.shape[0] > 1
    return pl.pallas_call(
        _ffn_kernel,
        grid=(b, s // tm, hidden // th),
        in_specs=[pl.BlockSpec((1, tm, d), lambda bi, i, j: (bi, i, 0)),
                  pl.BlockSpec((1, N_MOD, d),
                               (lambda bi, i, j: (bi, 0, 0)) if per_batch else (lambda bi, i, j: (0, 0, 0))),
                  pl.BlockSpec((1, d), lambda bi, i, j: (0, 0)),
                  pl.BlockSpec((1, d), lambda bi, i, j: (0, 0)),
                  pl.BlockSpec((d, th), lambda bi, i, j: (0, j)),
                  pl.BlockSpec((d, th), lambda bi, i, j: (0, j)),
                  pl.BlockSpec((th, d), lambda bi, i, j: (j, 0))],
        out_specs=pl.BlockSpec((1, tm, d), lambda bi, i, j: (bi, i, 0)),
        out_shape=jax.ShapeDtypeStruct((b, s, d), F32),
        scratch_shapes=[pltpu.VMEM((tm, d), BF16), pltpu.VMEM((tm, d), F32)],
        compiler_params=_params("arbitrary", "arbitrary", "arbitrary"),
        name="ffn",
    )(h, mods, g_pre, g_post, wg, wu, wd)


def _rope_tables(n_tokens, hd):
    freqs = hd // 4
    pos = jnp.arange(n_tokens)
    rowcol = jnp.stack([(pos // GRID_W).astype(F32), (pos % GRID_W).astype(F32)], axis=1)
    inv = ROPE_BASE ** (-jnp.arange(freqs, dtype=F32) / freqs)
    lane = jnp.arange(LANES) % hd
    axis, half, f = lane // (2 * freqs), (lane % (2 * freqs)) // freqs, lane % freqs
    ang = rowcol[:, axis] * inv[f][None, :]
    cos, sin = jnp.cos(ang), jnp.sin(ang)
    return cos, jnp.where(half == 1, sin, 0.0), jnp.where(half == 0, -sin, 0.0)


def kernel(x, c, ctx, c_ctx, w_ada, b_ada, g_pre_mix, g_post_mix, g_pre_ffn, g_post_ffn, w_in, diff_lambda, diff_subln, conv_w, conv_b, rnn_conv_w, rnn_conv_b, rg_wa, rg_ba, rg_wx, rg_bx, rg_lambda, b_merge, w_branch_a, w_branch_b, w_branch_c, w_o, w_ffn_gate, w_ffn_up, w_ffn_down):
    b, s, d = x.shape
    depth = w_ada.shape[0]
    hd = diff_lambda.shape[-1]
    diff_w = w_branch_a.shape[1]
    conv_width = conv_w.shape[-1]
    rnn_width = rnn_conv_w.shape[-1]
    nblk, bw = rg_wa.shape[2], rg_wa.shape[3]
    assert diff_w == conv_width == rnn_width and 2 * hd == LANES
    gate_col_rnn = 3

    rows = -(-(b + 1) // SUBLANES) * SUBLANES
    cc = jnp.zeros((rows, d), F32).at[:b].set(c).at[b].set(c_ctx)
    mods = _ada(cc, w_ada, b_ada.reshape(depth, 1, N_MOD * d))
    tables = _rope_tables(s, hd)
    q_scale = hd ** -0.5 * math.log2(math.e)

    vec = lambda a: a.reshape(2, nblk, 1, bw)
    h_lat, h_ctx = x, ctx
    for l in range(depth):
        need_ctx = l < depth - 1
        lam_init = 0.8 - 0.6 * math.exp(-0.3 * l)
        li = jnp.full((1, 1), lam_init, F32)
        ml = mods[l, :b].reshape(b, N_MOD, d)
        mc = mods[l, b:b + 1].reshape(1, N_MOD, d)
        w_in_l = w_in[l].astype(BF16)
        g_pre = g_pre_mix[l].reshape(1, d)
        subln = diff_subln[l].reshape(2 * hd, 1)

        qk_l, vt_l, rest_l, gates_l = _inproj(h_lat, ml, g_pre, w_in_l, tables, width=diff_w, q_scale=q_scale)
        qk_c, vt_c, rest_c, gates_c = _inproj(h_ctx, mc, g_pre, w_in_l, None, width=diff_w, q_scale=q_scale)

        ya_l = _attn(qk_l, qk_c, vt_c, qk_l, vt_l, diff_lambda[l], subln, li, hd=hd)

        prep = functools.partial(_prep, conv_w=conv_w[l], conv_b=conv_b[l].reshape(1, -1),
                                 rnn_conv_w=rnn_conv_w[l], rnn_conv_b=rnn_conv_b[l].reshape(1, -1),
                                 conv_width=conv_width, rnn_width=rnn_width)
        yb_l, xr_l = prep(rest_l)
        yb_c, xr_c = prep(rest_c)

        scan = functools.partial(_scan, wa=rg_wa[l], wx=rg_wx[l], ba=vec(rg_ba[l]), bx=vec(rg_bx[l]),
                                 lam=vec(rg_lambda[l]), gate_col=gate_col_rnn)
        zero = jnp.zeros((b, rnn_width), F32)
        hf_c, hfin_f = scan(xr_c, h0=zero, hf=None, rest=None, direction=0)
        yc_c, hfin_b = scan(xr_c, h0=zero, hf=hf_c, rest=rest_c, direction=1)
        hf_l, _ = scan(xr_l, h0=hfin_f, hf=None, rest=None, direction=0)
        yc_l, _ = scan(xr_l, h0=hfin_b, hf=hf_l, rest=rest_l, direction=1)

        wba, wbb, wbc = w_branch_a[l].astype(BF16), w_branch_b[l].astype(BF16), w_branch_c[l].astype(BF16)
        wo = w_o[l].astype(BF16)
        wg, wu, wd = w_ffn_gate[l].astype(BF16), w_ffn_up[l].astype(BF16), w_ffn_down[l].astype(BF16)
        g_post = g_post_mix[l].reshape(1, d)
        gf_pre, gf_post = g_pre_ffn[l].reshape(1, d), g_post_ffn[l].reshape(1, d)

        h_lat = _merge(ya_l, yb_l, yc_l, gates_l, b_merge[l], wba, wbb, wbc, wo, h_lat, ml, g_post)
        h_lat = _ffn(h_lat, ml, gf_pre, gf_post, wg, wu, wd)
        if need_ctx:
            ya_c = _attn(qk_c, qk_c, vt_c, None, None, diff_lambda[l], subln, li, hd=hd)
            h_ctx = _merge(ya_c, yb_c, yc_c, gates_c, b_merge[l], wba, wbb, wbc, wo, h_ctx, mc, g_post)
            h_ctx = _ffn(h_ctx, mc, gf_pre, gf_post, wg, wu, wd)
    return h_lat
```

```python
import functools
import math

import jax
import jax.numpy as jnp
from jax import lax
from jax.experimental import pallas as pl
from jax.experimental.pallas import tpu as pltpu

GRID_W = 64
ROPE_BASE = 10000.0
EPS = 1e-6
RG_C = 8.0
N_MOD = 6
N_BRANCH = 3
N_MID = 5
LANES = 128
SUBLANES = 8
VMEM_LIMIT_BYTES = 56 * 1024 * 1024
INPROJ_CHUNK = 512
SCAN_ROW_PAD = 8

F32 = jnp.float32
BF16 = jnp.bfloat16


def _tile(n, pref):
    if n <= pref:
        return n
    t = pref - pref % SUBLANES
    while t >= SUBLANES:
        if n % t == 0:
            return t
        t -= SUBLANES
    return n


def _params(*sem):
    return pltpu.CompilerParams(dimension_semantics=sem, vmem_limit_bytes=VMEM_LIMIT_BYTES)


def _rms(x, g):
    return x * lax.rsqrt(jnp.mean(x * x, axis=-1, keepdims=True) + EPS) * g


def _silu(x):
    return x * jax.nn.sigmoid(x)


def _ada_kernel(x_ref, w_ref, b_ref, o_ref):
    sx = _silu(x_ref[...]).astype(BF16)
    o_ref[0] = jnp.dot(sx, w_ref[0].astype(BF16), preferred_element_type=F32) + b_ref[0]


def _ada(cc, w_ada, b_flat):
    depth, d, n = w_ada.shape
    rows = cc.shape[0]
    tn = _tile(n, 1024)
    return pl.pallas_call(
        _ada_kernel,
        grid=(depth, n // tn),
        in_specs=[
            pl.BlockSpec((rows, d), lambda l, j: (0, 0)),
            pl.BlockSpec((1, d, tn), lambda l, j: (l, 0, j)),
            pl.BlockSpec((1, 1, tn), lambda l, j: (l, 0, j)),
        ],
        out_specs=pl.BlockSpec((1, rows, tn), lambda l, j: (l, 0, j)),
        out_shape=jax.ShapeDtypeStruct((depth, rows, n), F32),
        compiler_params=_params("arbitrary", "arbitrary"),
        name="ada",
    )(cc, w_ada, b_flat)


def _inproj_kernel(h_ref, mod_ref, g_ref, w_ref, *rest, rope, q_scale):
    if rope:
        cos_ref, s1_ref, s2_ref, qkv_ref, vt_ref, mid_ref, gates_ref, xn_ref = rest
    else:
        qkv_ref, vt_ref, mid_ref, gates_ref, xn_ref = rest
    j = pl.program_id(2)

    @pl.when(j == 0)
    def _():
        m = mod_ref[0]
        y = _rms(h_ref[0], g_ref[...])
        xn_ref[...] = (y * (1.0 + m[1:2]) + m[0:1]).astype(BF16)

    tn = w_ref.shape[1]
    cw = INPROJ_CHUNK if tn % INPROJ_CHUNK == 0 else tn

    def chunked(epilogue):
        for c in range(tn // cw):
            epilogue(c * cw, jnp.dot(xn_ref[...], w_ref[:, c * cw:(c + 1) * cw], preferred_element_type=F32))

    def qk_epilogue(c0, acc):
        a = acc * jnp.where(j == 0, q_scale, 1.0)
        if rope:
            cos, s1, s2 = cos_ref[...], s1_ref[...], s2_ref[...]
            quarter = LANES // 8
            for c in range(cw // LANES):
                blk = a[:, c * LANES:(c + 1) * LANES]
                r = blk * cos + pltpu.roll(blk, quarter, 1) * s1 + pltpu.roll(blk, LANES - quarter, 1) * s2
                qkv_ref[0, :, c0 + c * LANES:c0 + (c + 1) * LANES] = r.astype(BF16)
        else:
            qkv_ref[0, :, c0:c0 + cw] = a.astype(BF16)

    def v_epilogue(c0, acc):
        vt_ref[0, c0:c0 + cw, :] = acc.T.astype(BF16)

    def mid_epilogue(c0, acc):
        mid_ref[0, :, c0:c0 + cw] = acc.astype(BF16)

    def gates_epilogue(c0, acc):
        gates_ref[0, :, c0:c0 + cw] = acc.astype(BF16)

    pl.when(j < 2)(lambda: chunked(qk_epilogue))
    pl.when(j == 2)(lambda: chunked(v_epilogue))
    pl.when((j >= 3) & (j < 3 + N_MID))(lambda: chunked(mid_epilogue))
    pl.when(j >= 3 + N_MID)(lambda: chunked(gates_epilogue))


def _inproj(h, mods, g, w, tables, *, width, q_scale):
    b, s, d = h.shape
    n = w.shape[1]
    tn = width
    assert n == (3 + N_MID) * width + N_BRANCH * d and d % tn == 0
    tm = _tile(s, 1024)
    rope = tables is not None
    per_batch = mods.shape[0] > 1
    in_specs = [
        pl.BlockSpec((1, tm, d), lambda bi, i, j: (bi, i, 0)),
        pl.BlockSpec((1, N_MOD, d), (lambda bi, i, j: (bi, 0, 0)) if per_batch else (lambda bi, i, j: (0, 0, 0))),
        pl.BlockSpec((1, d), lambda bi, i, j: (0, 0)),
        pl.BlockSpec((d, tn), lambda bi, i, j: (0, j)),
    ]
    args = [h, mods, g, w]
    if rope:
        in_specs += [pl.BlockSpec((tm, LANES), lambda bi, i, j: (i, 0))] * 3
        args += list(tables)
    return pl.pallas_call(
        functools.partial(_inproj_kernel, rope=rope, q_scale=q_scale),
        grid=(b, s // tm, n // tn),
        in_specs=in_specs,
        out_specs=[
            pl.BlockSpec((1, tm, tn), lambda bi, i, j: (bi, i, jnp.minimum(j, 1))),
            pl.BlockSpec((1, tn, tm), lambda bi, i, j: (bi, 0, i)),
            pl.BlockSpec((1, tm, tn), lambda bi, i, j: (bi, i, jnp.clip(j - 3, 0, N_MID - 1))),
            pl.BlockSpec((1, tm, tn), lambda bi, i, j: (bi, i, jnp.maximum(j - 3 - N_MID, 0))),
        ],
        out_shape=[
            jax.ShapeDtypeStruct((b, s, 2 * width), BF16),
            jax.ShapeDtypeStruct((b, width, s), BF16),
            jax.ShapeDtypeStruct((b, s, N_MID * width), BF16),
            jax.ShapeDtypeStruct((b, s, N_BRANCH * d), BF16),
        ],
        scratch_shapes=[pltpu.VMEM((tm, d), BF16)],
        compiler_params=_params("arbitrary", "arbitrary", "arbitrary"),
        name="inproj_rope" if rope else "inproj",
    )(*args)


def _attn_kernel(*refs, heads, hd, has_lat):
    if has_lat:
        (q_ref, kc_ref, vc_ref, kl_ref, vl_ref, dl_ref, sg_ref, li_ref, o_ref,
         qs_ref, m_ref, l_ref, acc_ref, s_ref) = refs
    else:
        q_ref, kc_ref, vc_ref, dl_ref, sg_ref, li_ref, o_ref, qs_ref, m_ref, l_ref, acc_ref, s_ref = refs
    j = pl.program_id(2)
    last = pl.num_programs(2) - 1
    tq = q_ref.shape[1]
    hw = 2 * hd

    @pl.when(j == 0)
    def _():
        lane = lax.broadcasted_iota(jnp.int32, (tq, hw), 1)
        for h in range(heads):
            q = q_ref[0, :, h * hw:(h + 1) * hw].astype(F32)
            qs_ref[h, 0:tq] = jnp.where(lane < hd, q, 0.0).astype(BF16)
            qs_ref[h, tq:2 * tq] = jnp.where(lane >= hd, q, 0.0).astype(BF16)
        m_ref[...] = jnp.full(m_ref.shape, -jnp.inf, F32)
        l_ref[...] = jnp.zeros(l_ref.shape, F32)
        acc_ref[...] = jnp.zeros(acc_ref.shape, F32)

    def step(k_ref, vt_ref):
        tk = k_ref.shape[1]

        def scores(h):
            k = k_ref[0, :, h * hw:(h + 1) * hw]
            s_ref[h % 2, 0:tk, :] = lax.dot_general(k, qs_ref[h], (((1,), (1,)), ((), ())),
                                                    preferred_element_type=F32)

        scores(0)
        for h in range(heads):
            if h + 1 < heads:
                scores(h + 1)
            vt = vt_ref[0, h * hw:(h + 1) * hw, :]
            s = s_ref[h % 2, 0:tk, :]
            m_prev = m_ref[h]
            m_new = jnp.maximum(m_prev, jnp.max(s, axis=0, keepdims=True))
            alpha = jnp.exp2(m_prev - m_new)
            p = jnp.exp2(s - m_new)
            l_ref[h] = alpha * l_ref[h] + jnp.sum(p, axis=0, keepdims=True)
            acc_ref[h] = alpha * acc_ref[h] + jnp.dot(vt, p.astype(BF16), preferred_element_type=F32)
            m_ref[h] = m_new

    if has_lat:
        @pl.when(j == 0)
        def _():
            step(kc_ref, vc_ref)

        @pl.when(j > 0)
        def _():
            step(kl_ref, vl_ref)
    else:
        step(kc_ref, vc_ref)

    @pl.when(j == last)
    def _():
        dl = dl_ref[...]
        lam_init = li_ref[...]
        lam = (jnp.exp(jnp.sum(dl[0:1] * dl[1:2], axis=-1, keepdims=True))
               - jnp.exp(jnp.sum(dl[2:3] * dl[3:4], axis=-1, keepdims=True)) + lam_init)
        for h in range(heads):
            acc = acc_ref[h]
            l = l_ref[h]
            o = acc[:, 0:tq] / l[:, 0:tq] - lam * (acc[:, tq:2 * tq] / l[:, tq:2 * tq])
            o = o * lax.rsqrt(jnp.mean(o * o, axis=0, keepdims=True) + EPS) * sg_ref[...] * (1.0 - lam_init)
            o_ref[0, :, h * hw:(h + 1) * hw] = o.T.astype(BF16)


def _attn(qk_q, qk_c, vt_c, qk_l, vt_l, diff_lambda, subln, lam_init, *, hd):
    b, sq, w2 = qk_q.shape
    width = w2 // 2
    heads = width // (2 * hd)
    n_ctx = qk_c.shape[1]
    has_lat = qk_l is not None
    tq = _tile(sq, 512)
    in_specs = [
        pl.BlockSpec((1, tq, width), lambda bi, i, j: (bi, i, 0)),
        pl.BlockSpec((1, n_ctx, width), lambda bi, i, j: (bi, 0, 1)),
        pl.BlockSpec((1, width, n_ctx), lambda bi, i, j: (bi, 0, 0)),
    ]
    args = [qk_q, qk_c, vt_c]
    nkv = 1
    tk = n_ctx
    if has_lat:
        sk = qk_l.shape[1]
        tk = _tile(sk, 1024)
        assert tk >= n_ctx
        nkv += sk // tk
        in_specs += [
            pl.BlockSpec((1, tk, width), lambda bi, i, j: (bi, jnp.maximum(j - 1, 0), 1)),
            pl.BlockSpec((1, width, tk), lambda bi, i, j: (bi, 0, jnp.maximum(j - 1, 0))),
        ]
        args += [qk_l, vt_l]
    in_specs += [
        pl.BlockSpec(diff_lambda.shape, lambda bi, i, j: (0, 0)),
        pl.BlockSpec((2 * hd, 1), lambda bi, i, j: (0, 0)),
        pl.BlockSpec((1, 1), lambda bi, i, j: (0, 0)),
    ]
    args += [diff_lambda, subln, lam_init]
    return pl.pallas_call(
        functools.partial(_attn_kernel, heads=heads, hd=hd, has_lat=has_lat),
        grid=(b, sq // tq, nkv),
        in_specs=in_specs,
        out_specs=pl.BlockSpec((1, tq, width), lambda bi, i, j: (bi, i, 0)),
        out_shape=jax.ShapeDtypeStruct((b, sq, width), BF16),
        scratch_shapes=[
            pltpu.VMEM((heads, 2 * tq, 2 * hd), BF16),
            pltpu.VMEM((heads, 1, 2 * tq), F32),
            pltpu.VMEM((heads, 1, 2 * tq), F32),
            pltpu.VMEM((heads, 2 * hd, 2 * tq), F32),
            pltpu.VMEM((2, tk, 2 * tq), F32),
        ],
        compiler_params=_params("arbitrary", "arbitrary", "arbitrary"),
        name="attn_lat" if has_lat else "attn_ctx",
    )(*args)


def _prep_kernel(x3_ref, x4_ref, x5_ref, x7_ref, cw_ref, cb_ref, rw_ref, rb_ref, yb_ref, xr_ref):
    s = x3_ref.shape[1]
    row = lax.broadcasted_iota(jnp.int32, (s, x3_ref.shape[2]), 0)

    def shifted(x, k):
        r = pltpu.roll(x, k % s, 0)
        return jnp.where((row >= k) & (row < s + k), r, 0.0)

    z = x4_ref[0].astype(F32) * x3_ref[0].astype(F32)
    cw = cw_ref[...]
    conv = cw[0:1] * shifted(z, 1) + cw[1:2] * z + cw[2:3] * shifted(z, -1) + cb_ref[...]
    yb_ref[0] = (x5_ref[0].astype(F32) * conv).astype(BF16)
    x = x7_ref[0].astype(F32)
    rw = rw_ref[...]
    xr_ref[0] = (rw[0:1] * shifted(x, 2) + rw[1:2] * shifted(x, 1) + rw[2:3] * x + rw[3:4] * shifted(x, -1)
                 + rb_ref[...])


def _prep(rest, conv_w, conv_b, rnn_conv_w, rnn_conv_b, *, conv_width, rnn_width):
    b, s, _ = rest.shape
    assert conv_width == rnn_width
    tc = LANES
    nct = conv_width // tc
    col = lambda k: (lambda bi, c: (bi, 0, k * nct + c))
    par = lambda rows: pl.BlockSpec((rows, tc), lambda bi, c: (0, c))
    return pl.pallas_call(
        _prep_kernel,
        grid=(b, nct),
        in_specs=[pl.BlockSpec((1, s, tc), col(0)), pl.BlockSpec((1, s, tc), col(1)),
                  pl.BlockSpec((1, s, tc), col(2)), pl.BlockSpec((1, s, tc), col(4)),
                  par(conv_w.shape[0]), par(1), par(rnn_conv_w.shape[0]), par(1)],
        out_specs=[pl.BlockSpec((1, s, tc), lambda bi, c: (bi, 0, c)),
                   pl.BlockSpec((1, s, tc), lambda bi, c: (bi, 0, c))],
        out_shape=[jax.ShapeDtypeStruct((b, s, conv_width), BF16),
                   jax.ShapeDtypeStruct((b, s, rnn_width), F32)],
        compiler_params=_params("arbitrary", "arbitrary"),
        name="conv_prep",
    )(rest, rest, rest, rest, conv_w, conv_b, rnn_conv_w, rnn_conv_b)


def _scan_kernel(*refs, reverse, finalize):
    if finalize:
        (xr_ref, wa_ref, wx_ref, ba_ref, bx_ref, lam_ref, h0_ref, hf_ref, gate_ref, out_ref, hlast_ref,
         a_s, g_s, h_s, carry) = refs
    else:
        xr_ref, wa_ref, wx_ref, ba_ref, bx_ref, lam_ref, h0_ref, out_ref, hlast_ref, a_s, g_s, h_s, carry = refs
    i = pl.program_id(1)
    nb, tc, cw = xr_ref.shape

    @pl.when(i == 0)
    def _():
        carry[...] = h0_ref[...]

    x = xr_ref[...].reshape(nb * tc, cw)
    xb = x.astype(BF16)
    r = jax.nn.sigmoid(jnp.dot(xb, wa_ref[0, 0].astype(BF16), preferred_element_type=F32) + ba_ref[0, 0])
    gi = jax.nn.sigmoid(jnp.dot(xb, wx_ref[0, 0].astype(BF16), preferred_element_type=F32) + bx_ref[0, 0])
    z = -lam_ref[0, 0]
    softplus = jnp.maximum(z, 0.0) + jnp.log1p(jnp.exp(-jnp.abs(z)))
    a = jnp.exp(-RG_C * r * softplus)
    g = jnp.sqrt(1.0 - a * a) * (gi * x)
    pitch = tc + SCAN_ROW_PAD
    for bi in range(nb):
        a_s[bi * pitch:bi * pitch + tc, :] = a[bi * tc:(bi + 1) * tc]
        g_s[bi * pitch:bi * pitch + tc, :] = g[bi * tc:(bi + 1) * tc]

    def body(t, h):
        tt = tc - 1 - t if reverse else t
        rows = pl.ds(tt, nb, stride=pitch)
        h = a_s[rows, :] * h + g_s[rows, :]
        h_s[rows, :] = h
        return h

    h = lax.fori_loop(0, tc, body, carry[...], unroll=8)
    carry[...] = h
    hlast_ref[...] = h
    for bi in range(nb):
        hs = h_s[bi * pitch:bi * pitch + tc, :]
        if finalize:
            out_ref[bi] = (jax.nn.gelu(gate_ref[bi].astype(F32)) * (hf_ref[bi] + hs)).astype(out_ref.dtype)
        else:
            out_ref[bi] = hs


def _scan(xr, wa, wx, ba, bx, lam, h0, hf, rest, *, direction, gate_col):
    b, s, c = xr.shape
    nblk, bw = wa.shape[1], wa.shape[2]
    assert bw == LANES and nblk * bw == c
    tc = _tile(s, 512)
    nchunk = s // tc
    reverse = direction == 1
    finalize = hf is not None
    chunk = (lambda i: nchunk - 1 - i) if reverse else (lambda i: i)
    d = direction
    seq_spec = pl.BlockSpec((b, tc, bw), lambda n, i: (0, chunk(i), n))
    w_spec = pl.BlockSpec((1, 1, bw, bw), lambda n, i: (d, n, 0, 0))
    v_spec = pl.BlockSpec((1, 1, 1, bw), lambda n, i: (d, n, 0, 0))
    st_spec = pl.BlockSpec((b, bw), lambda n, i: (0, n))
    in_specs = [seq_spec, w_spec, w_spec, v_spec, v_spec, v_spec, st_spec]
    args = [xr, wa, wx, ba, bx, lam, h0]
    if finalize:
        in_specs += [seq_spec, pl.BlockSpec((b, tc, bw), lambda n, i: (0, chunk(i), gate_col * nblk + n))]
        args += [hf, rest]
    return pl.pallas_call(
        functools.partial(_scan_kernel, reverse=reverse, finalize=finalize),
        grid=(nblk, nchunk),
        in_specs=in_specs,
        out_specs=[seq_spec, st_spec],
        out_shape=[jax.ShapeDtypeStruct((b, s, c), BF16 if finalize else F32),
                   jax.ShapeDtypeStruct((b, c), F32)],
        scratch_shapes=[pltpu.VMEM((b * (tc + SCAN_ROW_PAD), bw), F32)] * 3 + [pltpu.VMEM((b, bw), F32)],
        compiler_params=_params("arbitrary", "arbitrary"),
        name="rglru_bwd" if reverse else "rglru_fwd",
    )(*args)


def _merge_kernel(ya_ref, yb_ref, yc_ref, g0_ref, g1_ref, g2_ref, bm_ref, wa_ref, wb_ref, wc_ref, wo_ref,
                  h_ref, mod_ref, gp_ref, o_ref):
    bm = bm_ref[...]
    gate = lambda g_ref, k: jax.nn.sigmoid(g_ref[0].astype(F32) + bm[k:k + 1])
    m = gate(g0_ref, 0) * jnp.dot(ya_ref[0], wa_ref[...], preferred_element_type=F32)
    m += gate(g1_ref, 1) * jnp.dot(yb_ref[0], wb_ref[...], preferred_element_type=F32)
    m += gate(g2_ref, 2) * jnp.dot(yc_ref[0], wc_ref[...], preferred_element_type=F32)
    out = jnp.dot(m.astype(BF16), wo_ref[...], preferred_element_type=F32)
    o_ref[0] = h_ref[0] + mod_ref[0][2:3] * _rms(out, gp_ref[...])


def _merge(ya, yb, yc, gates, b_merge, wba, wbb, wbc, wo, h, mods, g_post):
    b, s, d = h.shape
    tm = _tile(s, 256)
    per_batch = mods.shape[0] > 1
    row = lambda w: pl.BlockSpec((1, tm, w), lambda bi, i: (bi, i, 0))
    gate = lambda k: pl.BlockSpec((1, tm, d), lambda bi, i: (bi, i, k))
    full = lambda a: pl.BlockSpec(a.shape, lambda bi, i: (0, 0))
    return pl.pallas_call(
        _merge_kernel,
        grid=(b, s // tm),
        in_specs=[row(ya.shape[2]), row(yb.shape[2]), row(yc.shape[2]), gate(0), gate(1), gate(2),
                  full(b_merge), full(wba), full(wbb), full(wbc), full(wo), row(d),
                  pl.BlockSpec((1, N_MOD, d), (lambda bi, i: (bi, 0, 0)) if per_batch else (lambda bi, i: (0, 0, 0))),
                  pl.BlockSpec((1, d), lambda bi, i: (0, 0))],
        out_specs=row(d),
        out_shape=jax.ShapeDtypeStruct((b, s, d), F32),
        compiler_params=_params("arbitrary", "arbitrary"),
        name="merge",
    )(ya, yb, yc, gates, gates, gates, b_merge, wba, wbb, wbc, wo, h, mods, g_post)


def _ffn_kernel(h_ref, mod_ref, gpre_ref, gpost_ref, wg_ref, wu_ref, wd_ref, o_ref, un_ref, acc_ref):
    j = pl.program_id(2)

    @pl.when(j == 0)
    def _():
        m = mod_ref[0]
        un_ref[...] = (_rms(h_ref[0], gpre_ref[...]) * (1.0 + m[4:5]) + m[3:4]).astype(BF16)
        acc_ref[...] = jnp.zeros(acc_ref.shape, F32)

    u = un_ref[...]
    hg = jnp.dot(u, wg_ref[...], preferred_element_type=F32)
    hu = jnp.dot(u, wu_ref[...], preferred_element_type=F32)
    acc_ref[...] += jnp.dot((_silu(hg) * hu).astype(BF16), wd_ref[...], preferred_element_type=F32)

    @pl.when(j == pl.num_programs(2) - 1)
    def _():
        o_ref[0] = h_ref[0] + mod_ref[0][5:6] * _rms(acc_ref[...], gpost_ref[...])


def _ffn(h, mods, g_pre, g_post, wg, wu, wd):
    b, s, d = h.shape
    hidden = wg.shape[1]
    tm = _tile(s, 512)
    th = _tile(hidden, 512)
    per_batch = mods.shape[0] > 1
    return pl.pallas_call(
        _ffn_kernel,
        grid=(b, s // tm, hidden // th),
        in_specs=[pl.BlockSpec((1, tm, d), lambda bi, i, j: (bi, i, 0)),
                  pl.BlockSpec((1, N_MOD, d),
                               (lambda bi, i, j: (bi, 0, 0)) if per_batch else (lambda bi, i, j: (0, 0, 0))),
                  pl.BlockSpec((1, d), lambda bi, i, j: (0, 0)),
                  pl.BlockSpec((1, d), lambda bi, i, j: (0, 0)),
                  pl.BlockSpec((d, th), lambda bi, i, j: (0, j)),
                  pl.BlockSpec((d, th), lambda bi, i, j: (0, j)),
                  pl.BlockSpec((th, d), lambda bi, i, j: (j, 0))],
        out_specs=pl.BlockSpec((1, tm, d), lambda bi, i, j: (bi, i, 0)),
        out_shape=jax.ShapeDtypeStruct((b, s, d), F32),
        scratch_shapes=[pltpu.VMEM((tm, d), BF16), pltpu.VMEM((tm, d), F32)],
        compiler_params=_params("arbitrary", "arbitrary", "arbitrary"),
        name="ffn",
    )(h, mods, g_pre, g_post, wg, wu, wd)


def _rope_tables(n_tokens, hd):
    freqs = hd // 4
    pos = jnp.arange(n_tokens)
    rowcol = jnp.stack([(pos // GRID_W).astype(F32), (pos % GRID_W).astype(F32)], axis=1)
    inv = ROPE_BASE ** (-jnp.arange(freqs, dtype=F32) / freqs)
    lane = jnp.arange(LANES) % hd
    axis, half, f = lane // (2 * freqs), (lane % (2 * freqs)) // freqs, lane % freqs
    ang = rowcol[:, axis] * inv[f][None, :]
    cos, sin = jnp.cos(ang), jnp.sin(ang)
    return cos, jnp.where(half == 1, sin, 0.0), jnp.where(half == 0, -sin, 0.0)


def kernel(x, c, ctx, c_ctx, w_ada, b_ada, g_pre_mix, g_post_mix, g_pre_ffn, g_post_ffn, w_in, diff_lambda, diff_subln, conv_w, conv_b, rnn_conv_w, rnn_conv_b, rg_wa, rg_ba, rg_wx, rg_bx, rg_lambda, b_merge, w_branch_a, w_branch_b, w_branch_c, w_o, w_ffn_gate, w_ffn_up, w_ffn_down):
    b, s, d = x.shape
    depth = w_ada.shape[0]
    hd = diff_lambda.shape[-1]
    diff_w = w_branch_a.shape[1]
    conv_width = conv_w.shape[-1]
    rnn_width = rnn_conv_w.shape[-1]
    nblk, bw = rg_wa.shape[2], rg_wa.shape[3]
    assert diff_w == conv_width == rnn_width and 2 * hd == LANES
    gate_col_rnn = 3

    rows = -(-(b + 1) // SUBLANES) * SUBLANES
    cc = jnp.zeros((rows, d), F32).at[:b].set(c).at[b].set(c_ctx)
    mods = _ada(cc, w_ada, b_ada.reshape(depth, 1, N_MOD * d))
    tables = _rope_tables(s, hd)
    q_scale = hd ** -0.5 * math.log2(math.e)

    vec = lambda a: a.reshape(2, nblk, 1, bw)
    h_lat, h_ctx = x, ctx
    for l in range(depth):
        need_ctx = l < depth - 1
        lam_init = 0.8 - 0.6 * math.exp(-0.3 * l)
        li = jnp.full((1, 1), lam_init, F32)
        ml = mods[l, :b].reshape(b, N_MOD, d)
        mc = mods[l, b:b + 1].reshape(1, N_MOD, d)
        w_in_l = w_in[l].astype(BF16)
        g_pre = g_pre_mix[l].reshape(1, d)
        subln = diff_subln[l].reshape(2 * hd, 1)

        qk_l, vt_l, rest_l, gates_l = _inproj(h_lat, ml, g_pre, w_in_l, tables, width=diff_w, q_scale=q_scale)
        qk_c, vt_c, rest_c, gates_c = _inproj(h_ctx, mc, g_pre, w_in_l, None, width=diff_w, q_scale=q_scale)

        ya_l = _attn(qk_l, qk_c, vt_c, qk_l, vt_l, diff_lambda[l], subln, li, hd=hd)

        prep = functools.partial(_prep, conv_w=conv_w[l], conv_b=conv_b[l].reshape(1, -1),
                                 rnn_conv_w=rnn_conv_w[l], rnn_conv_b=rnn_conv_b[l].reshape(1, -1),
                                 conv_width=conv_width, rnn_width=rnn_width)
        yb_l, xr_l = prep(rest_l)
        yb_c, xr_c = prep(rest_c)

        scan = functools.partial(_scan, wa=rg_wa[l], wx=rg_wx[l], ba=vec(rg_ba[l]), bx=vec(rg_bx[l]),
                                 lam=vec(rg_lambda[l]), gate_col=gate_col_rnn)
        zero = jnp.zeros((b, rnn_width), F32)
        hf_c, hfin_f = scan(xr_c, h0=zero, hf=None, rest=None, direction=0)
        yc_c, hfin_b = scan(xr_c, h0=zero, hf=hf_c, rest=rest_c, direction=1)
        hf_l, _ = scan(xr_l, h0=hfin_f, hf=None, rest=None, direction=0)
        yc_l, _ = scan(xr_l, h0=hfin_b, hf=hf_l, rest=rest_l, direction=1)

        wba, wbb, wbc = w_branch_a[l].astype(BF16), w_branch_b[l].astype(BF16), w_branch_c[l].astype(BF16)
        wo = w_o[l].astype(BF16)
        wg, wu, wd = w_ffn_gate[l].astype(BF16), w_ffn_up[l].astype(BF16), w_ffn_down[l].astype(BF16)
        g_post = g_post_mix[l].reshape(1, d)
        gf_pre, gf_post = g_pre_ffn[l].reshape(1, d), g_post_ffn[l].reshape(1, d)

        h_lat = _merge(ya_l, yb_l, yc_l, gates_l, b_merge[l], wba, wbb, wbc, wo, h_lat, ml, g_post)
        h_lat = _ffn(h_lat, ml, gf_pre, gf_post, wg, wu, wd)
        if need_ctx:
            ya_c = _attn(qk_c, qk_c, vt_c, None, None, diff_lambda[l], subln, li, hd=hd)
            h_ctx = _merge(ya_c, yb_c, yc_c, gates_c, b_merge[l], wba, wbb, wbc, wo, h_ctx, mc, g_post)
            h_ctx = _ffn(h_ctx, mc, gf_pre, gf_post, wg, wu, wd)
    return h_lat
```

```python
import functools
import math

import jax
import jax.numpy as jnp
from jax import lax
from jax.experimental import pallas as pl
from jax.experimental.pallas import tpu as pltpu

GRID_W = 64
ROPE_BASE = 10000.0
EPS = 1e-6
RG_C = 8.0
N_MOD = 6
N_BRANCH = 3
N_MID = 5
LANES = 128
SUBLANES = 8
VMEM_LIMIT_BYTES = 56 * 1024 * 1024
INPROJ_CHUNK = 512
ONES_ROWS = 16
SCAN_ROW_PAD = 8

F32 = jnp.float32
BF16 = jnp.bfloat16


def _tile(n, pref):
    if n <= pref:
        return n
    t = pref - pref % SUBLANES
    while t >= SUBLANES:
        if n % t == 0:
            return t
        t -= SUBLANES
    return n


def _params(*sem):
    return pltpu.CompilerParams(dimension_semantics=sem, vmem_limit_bytes=VMEM_LIMIT_BYTES)


def _rms(x, g):
    return x * lax.rsqrt(jnp.mean(x * x, axis=-1, keepdims=True) + EPS) * g


def _silu(x):
    return x * jax.nn.sigmoid(x)


def _ada_kernel(x_ref, w_ref, b_ref, o_ref):
    sx = _silu(x_ref[...]).astype(BF16)
    o_ref[0] = jnp.dot(sx, w_ref[0].astype(BF16), preferred_element_type=F32) + b_ref[0]


def _ada(cc, w_ada, b_flat):
    depth, d, n = w_ada.shape
    rows = cc.shape[0]
    tn = _tile(n, 1024)
    return pl.pallas_call(
        _ada_kernel,
        grid=(depth, n // tn),
        in_specs=[
            pl.BlockSpec((rows, d), lambda l, j: (0, 0)),
            pl.BlockSpec((1, d, tn), lambda l, j: (l, 0, j)),
            pl.BlockSpec((1, 1, tn), lambda l, j: (l, 0, j)),
        ],
        out_specs=pl.BlockSpec((1, rows, tn), lambda l, j: (l, 0, j)),
        out_shape=jax.ShapeDtypeStruct((depth, rows, n), F32),
        compiler_params=_params("arbitrary", "arbitrary"),
        name="ada",
    )(cc, w_ada, b_flat)


def _inproj_kernel(h_ref, mod_ref, g_ref, w_ref, *rest, rope, q_scale):
    if rope:
        cos_ref, s1_ref, s2_ref, qkv_ref, vt_ref, mid_ref, gates_ref, xn_ref = rest
    else:
        qkv_ref, vt_ref, mid_ref, gates_ref, xn_ref = rest
    j = pl.program_id(2)

    @pl.when(j == 0)
    def _():
        m = mod_ref[0]
        y = _rms(h_ref[0], g_ref[...])
        xn_ref[...] = (y * (1.0 + m[1:2]) + m[0:1]).astype(BF16)

    tn = w_ref.shape[1]
    cw = INPROJ_CHUNK if tn % INPROJ_CHUNK == 0 else tn

    def chunked(epilogue):
        for c in range(tn // cw):
            epilogue(c * cw, jnp.dot(xn_ref[...], w_ref[:, c * cw:(c + 1) * cw], preferred_element_type=F32))

    def qk_epilogue(c0, acc):
        a = acc * jnp.where(j == 0, q_scale, 1.0)
        if rope:
            cos, s1, s2 = cos_ref[...], s1_ref[...], s2_ref[...]
            quarter = LANES // 8
            for c in range(cw // LANES):
                blk = a[:, c * LANES:(c + 1) * LANES]
                r = blk * cos + pltpu.roll(blk, quarter, 1) * s1 + pltpu.roll(blk, LANES - quarter, 1) * s2
                qkv_ref[0, :, c0 + c * LANES:c0 + (c + 1) * LANES] = r.astype(BF16)
        else:
            qkv_ref[0, :, c0:c0 + cw] = a.astype(BF16)

    def v_epilogue(c0, acc):
        vt_ref[0, c0:c0 + cw, :] = acc.T.astype(BF16)

    def mid_epilogue(c0, acc):
        mid_ref[0, :, c0:c0 + cw] = acc.astype(BF16)

    def gates_epilogue(c0, acc):
        gates_ref[0, :, c0:c0 + cw] = acc.astype(BF16)

    pl.when(j < 2)(lambda: chunked(qk_epilogue))
    pl.when(j == 2)(lambda: chunked(v_epilogue))
    pl.when((j >= 3) & (j < 3 + N_MID))(lambda: chunked(mid_epilogue))
    pl.when(j >= 3 + N_MID)(lambda: chunked(gates_epilogue))


def _inproj(h, mods, g, w, tables, *, width, q_scale):
    b, s, d = h.shape
    n = w.shape[1]
    tn = width
    assert n == (3 + N_MID) * width + N_BRANCH * d and d % tn == 0
    tm = _tile(s, 1024)
    rope = tables is not None
    per_batch = mods.shape[0] > 1
    in_specs = [
        pl.BlockSpec((1, tm, d), lambda bi, i, j: (bi, i, 0)),
        pl.BlockSpec((1, N_MOD, d), (lambda bi, i, j: (bi, 0, 0)) if per_batch else (lambda bi, i, j: (0, 0, 0))),
        pl.BlockSpec((1, d), lambda bi, i, j: (0, 0)),
        pl.BlockSpec((d, tn), lambda bi, i, j: (0, j)),
    ]
    args = [h, mods, g, w]
    if rope:
        in_specs += [pl.BlockSpec((tm, LANES), lambda bi, i, j: (i, 0))] * 3
        args += list(tables)
    return pl.pallas_call(
        functools.partial(_inproj_kernel, rope=rope, q_scale=q_scale),
        grid=(b, s // tm, n // tn),
        in_specs=in_specs,
        out_specs=[
            pl.BlockSpec((1, tm, tn), lambda bi, i, j: (bi, i, jnp.minimum(j, 1))),
            pl.BlockSpec((1, tn, tm), lambda bi, i, j: (bi, 0, i)),
            pl.BlockSpec((1, tm, tn), lambda bi, i, j: (bi, i, jnp.clip(j - 3, 0, N_MID - 1))),
            pl.BlockSpec((1, tm, tn), lambda bi, i, j: (bi, i, jnp.maximum(j - 3 - N_MID, 0))),
        ],
        out_shape=[
            jax.ShapeDtypeStruct((b, s, 2 * width), BF16),
            jax.ShapeDtypeStruct((b, width, s), BF16),
            jax.ShapeDtypeStruct((b, s, N_MID * width), BF16),
            jax.ShapeDtypeStruct((b, s, N_BRANCH * d), BF16),
        ],
        scratch_shapes=[pltpu.VMEM((tm, d), BF16)],
        compiler_params=_params("arbitrary", "arbitrary", "arbitrary"),
        name="inproj_rope" if rope else "inproj",
    )(*args)


def _attn_kernel(*refs, heads, hd, has_lat):
    if has_lat:
        (q_ref, kc_ref, vc_ref, kl_ref, vl_ref, dl_ref, sg_ref, li_ref, o_ref,
         qs_ref, m_ref, acc_ref, s_ref) = refs
    else:
        q_ref, kc_ref, vc_ref, dl_ref, sg_ref, li_ref, o_ref, qs_ref, m_ref, acc_ref, s_ref = refs
    j = pl.program_id(2)
    last = pl.num_programs(2) - 1
    tq = q_ref.shape[1]
    hw = 2 * hd

    @pl.when(j == 0)
    def _():
        lane = lax.broadcasted_iota(jnp.int32, (tq, hw), 1)
        for h in range(heads):
            q = q_ref[0, :, h * hw:(h + 1) * hw].astype(F32)
            qs_ref[h, 0:tq] = jnp.where(lane < hd, q, 0.0).astype(BF16)
            qs_ref[h, tq:2 * tq] = jnp.where(lane >= hd, q, 0.0).astype(BF16)
        m_ref[...] = jnp.full(m_ref.shape, -jnp.inf, F32)
        acc_ref[...] = jnp.zeros(acc_ref.shape, F32)

    def step(k_ref, vt_ref):
        tk = k_ref.shape[1]
        ones = jnp.ones((ONES_ROWS, tk), BF16)

        def scores(h):
            k = k_ref[0, :, h * hw:(h + 1) * hw]
            s_ref[h % 2, 0:tk, :] = lax.dot_general(k, qs_ref[h], (((1,), (1,)), ((), ())),
                                                    preferred_element_type=F32)

        scores(0)
        for h in range(heads):
            if h + 1 < heads:
                scores(h + 1)
            vt = jnp.concatenate([vt_ref[0, h * hw:(h + 1) * hw, :], ones], axis=0)
            s = s_ref[h % 2, 0:tk, :]
            m_prev = m_ref[h]
            m_new = jnp.maximum(m_prev, jnp.max(s, axis=0, keepdims=True))
            alpha = jnp.exp2(m_prev - m_new)
            p = jnp.exp2(s - m_new)
            acc_ref[h] = alpha * acc_ref[h] + jnp.dot(vt, p.astype(BF16), preferred_element_type=F32)
            m_ref[h] = m_new

    if has_lat:
        @pl.when(j == 0)
        def _():
            step(kc_ref, vc_ref)

        @pl.when(j > 0)
        def _():
            step(kl_ref, vl_ref)
    else:
        step(kc_ref, vc_ref)

    @pl.when(j == last)
    def _():
        dl = dl_ref[...]
        lam_init = li_ref[...]
        lam = (jnp.exp(jnp.sum(dl[0:1] * dl[1:2], axis=-1, keepdims=True))
               - jnp.exp(jnp.sum(dl[2:3] * dl[3:4], axis=-1, keepdims=True)) + lam_init)
        for h in range(heads):
            acc = acc_ref[h, 0:hw]
            l = acc_ref[h, hw:hw + 1]
            o = acc[:, 0:tq] / l[:, 0:tq] - lam * (acc[:, tq:2 * tq] / l[:, tq:2 * tq])
            o = o * lax.rsqrt(jnp.mean(o * o, axis=0, keepdims=True) + EPS) * sg_ref[...] * (1.0 - lam_init)
            o_ref[0, :, h * hw:(h + 1) * hw] = o.T.astype(BF16)


def _attn(qk_q, qk_c, vt_c, qk_l, vt_l, diff_lambda, subln, lam_init, *, hd):
    b, sq, w2 = qk_q.shape
    width = w2 // 2
    heads = width // (2 * hd)
    n_ctx = qk_c.shape[1]
    has_lat = qk_l is not None
    tq = _tile(sq, 512)
    vc_map = (lambda bi, i, j: (bi, 0, 0)) if vt_c.shape[0] == b else (lambda bi, i, j: (0, 0, bi))
    in_specs = [
        pl.BlockSpec((1, tq, width), lambda bi, i, j: (bi, i, 0)),
        pl.BlockSpec((1, n_ctx, width), lambda bi, i, j: (bi, 0, 1)),
        pl.BlockSpec((1, width, n_ctx), vc_map),
    ]
    args = [qk_q, qk_c, vt_c]
    nkv = 1
    tk = n_ctx
    if has_lat:
        sk = qk_l.shape[1]
        tk = _tile(sk, 1024)
        assert tk >= n_ctx
        nkv += sk // tk
        in_specs += [
            pl.BlockSpec((1, tk, width), lambda bi, i, j: (bi, jnp.maximum(j - 1, 0), 1)),
            pl.BlockSpec((1, width, tk), lambda bi, i, j: (bi, 0, jnp.maximum(j - 1, 0))),
        ]
        args += [qk_l, vt_l]
    in_specs += [
        pl.BlockSpec(diff_lambda.shape, lambda bi, i, j: (0, 0)),
        pl.BlockSpec((2 * hd, 1), lambda bi, i, j: (0, 0)),
        pl.BlockSpec((1, 1), lambda bi, i, j: (0, 0)),
    ]
    args += [diff_lambda, subln, lam_init]
    return pl.pallas_call(
        functools.partial(_attn_kernel, heads=heads, hd=hd, has_lat=has_lat),
        grid=(b, sq // tq, nkv),
        in_specs=in_specs,
        out_specs=pl.BlockSpec((1, tq, width), lambda bi, i, j: (bi, i, 0)),
        out_shape=jax.ShapeDtypeStruct((b, sq, width), BF16),
        scratch_shapes=[
            pltpu.VMEM((heads, 2 * tq, 2 * hd), BF16),
            pltpu.VMEM((heads, 1, 2 * tq), F32),
            pltpu.VMEM((heads, 2 * hd + ONES_ROWS, 2 * tq), F32),
            pltpu.VMEM((2, tk, 2 * tq), F32),
        ],
        compiler_params=_params("arbitrary", "arbitrary", "arbitrary"),
        name="attn_lat" if has_lat else "attn_ctx",
    )(*args)


def _prep_kernel(x3_ref, x4_ref, x5_ref, x7_ref, cw_ref, cb_ref, rw_ref, rb_ref, yb_ref, xr_ref):
    s = x3_ref.shape[1]
    row = lax.broadcasted_iota(jnp.int32, (s, x3_ref.shape[2]), 0)

    def shifted(x, k):
        r = pltpu.roll(x, k % s, 0)
        return jnp.where((row >= k) & (row < s + k), r, 0.0)

    z = x4_ref[0].astype(F32) * x3_ref[0].astype(F32)
    cw = cw_ref[...]
    conv = cw[0:1] * shifted(z, 1) + cw[1:2] * z + cw[2:3] * shifted(z, -1) + cb_ref[...]
    yb_ref[0] = (x5_ref[0].astype(F32) * conv).astype(BF16)
    x = x7_ref[0].astype(F32)
    rw = rw_ref[...]
    xr_ref[0] = (rw[0:1] * shifted(x, 2) + rw[1:2] * shifted(x, 1) + rw[2:3] * x + rw[3:4] * shifted(x, -1)
                 + rb_ref[...])


def _prep(rest, conv_w, conv_b, rnn_conv_w, rnn_conv_b, *, conv_width, rnn_width):
    b, s, _ = rest.shape
    assert conv_width == rnn_width
    tc = LANES
    nct = conv_width // tc
    col = lambda k: (lambda bi, c: (bi, 0, k * nct + c))
    par = lambda rows: pl.BlockSpec((rows, tc), lambda bi, c: (0, c))
    return pl.pallas_call(
        _prep_kernel,
        grid=(b, nct),
        in_specs=[pl.BlockSpec((1, s, tc), col(0)), pl.BlockSpec((1, s, tc), col(1)),
                  pl.BlockSpec((1, s, tc), col(2)), pl.BlockSpec((1, s, tc), col(4)),
                  par(conv_w.shape[0]), par(1), par(rnn_conv_w.shape[0]), par(1)],
        out_specs=[pl.BlockSpec((1, s, tc), lambda bi, c: (bi, 0, c)),
                   pl.BlockSpec((1, s, tc), lambda bi, c: (bi, 0, c))],
        out_shape=[jax.ShapeDtypeStruct((b, s, conv_width), BF16),
                   jax.ShapeDtypeStruct((b, s, rnn_width), F32)],
        compiler_params=_params("arbitrary", "arbitrary"),
        name="conv_prep",
    )(rest, rest, rest, rest, conv_w, conv_b, rnn_conv_w, rnn_conv_b)


def _scan_kernel(*refs, reverse, finalize):
    if finalize:
        (xr_ref, wa_ref, wx_ref, ba_ref, bx_ref, lam_ref, h0_ref, hf_ref, gate_ref, out_ref, hlast_ref,
         a_s, g_s, h_s, carry) = refs
    else:
        xr_ref, wa_ref, wx_ref, ba_ref, bx_ref, lam_ref, h0_ref, out_ref, hlast_ref, a_s, g_s, h_s, carry = refs
    i = pl.program_id(1)
    nb, tc, cw = xr_ref.shape

    @pl.when(i == 0)
    def _():
        carry[...] = h0_ref[...]

    x = xr_ref[...].reshape(nb * tc, cw)
    xb = x.astype(BF16)
    r = jax.nn.sigmoid(jnp.dot(xb, wa_ref[0, 0].astype(BF16), preferred_element_type=F32) + ba_ref[0, 0])
    gi = jax.nn.sigmoid(jnp.dot(xb, wx_ref[0, 0].astype(BF16), preferred_element_type=F32) + bx_ref[0, 0])
    z = -lam_ref[0, 0]
    softplus = jnp.maximum(z, 0.0) + jnp.log1p(jnp.exp(-jnp.abs(z)))
    a = jnp.exp(-RG_C * r * softplus)
    g = jnp.sqrt(1.0 - a * a) * (gi * x)
    pitch = tc + SCAN_ROW_PAD
    for bi in range(nb):
        a_s[bi * pitch:bi * pitch + tc, :] = a[bi * tc:(bi + 1) * tc]
        g_s[bi * pitch:bi * pitch + tc, :] = g[bi * tc:(bi + 1) * tc]

    def body(t, h):
        tt = tc - 1 - t if reverse else t
        rows = pl.ds(tt, nb, stride=pitch)
        h = a_s[rows, :] * h + g_s[rows, :]
        h_s[rows, :] = h
        return h

    h = lax.fori_loop(0, tc, body, carry[...], unroll=8)
    carry[...] = h
    hlast_ref[...] = h
    for bi in range(nb):
        hs = h_s[bi * pitch:bi * pitch + tc, :]
        if finalize:
            out_ref[bi] = (jax.nn.gelu(gate_ref[bi].astype(F32)) * (hf_ref[bi] + hs)).astype(out_ref.dtype)
        else:
            out_ref[bi] = hs


def _scan(xr, wa, wx, ba, bx, lam, h0, hf, rest, *, direction, gate_col):
    b, s, c = xr.shape
    nblk, bw = wa.shape[1], wa.shape[2]
    assert bw == LANES and nblk * bw == c
    tc = _tile(s, 512)
    nchunk = s // tc
    reverse = direction == 1
    finalize = hf is not None
    chunk = (lambda i: nchunk - 1 - i) if reverse else (lambda i: i)
    d = direction
    seq_spec = pl.BlockSpec((b, tc, bw), lambda n, i: (0, chunk(i), n))
    w_spec = pl.BlockSpec((1, 1, bw, bw), lambda n, i: (d, n, 0, 0))
    v_spec = pl.BlockSpec((1, 1, 1, bw), lambda n, i: (d, n, 0, 0))
    st_spec = pl.BlockSpec((b, bw), lambda n, i: (0, n))
    in_specs = [seq_spec, w_spec, w_spec, v_spec, v_spec, v_spec, st_spec]
    args = [xr, wa, wx, ba, bx, lam, h0]
    if finalize:
        in_specs += [seq_spec, pl.BlockSpec((b, tc, bw), lambda n, i: (0, chunk(i), gate_col * nblk + n))]
        args += [hf, rest]
    return pl.pallas_call(
        functools.partial(_scan_kernel, reverse=reverse, finalize=finalize),
        grid=(nblk, nchunk),
        in_specs=in_specs,
        out_specs=[seq_spec, st_spec],
        out_shape=[jax.ShapeDtypeStruct((b, s, c), BF16 if finalize else F32),
                   jax.ShapeDtypeStruct((b, c), F32)],
        scratch_shapes=[pltpu.VMEM((b * (tc + SCAN_ROW_PAD), bw), F32)] * 3 + [pltpu.VMEM((b, bw), F32)],
        compiler_params=_params("arbitrary", "arbitrary"),
        name="rglru_bwd" if reverse else "rglru_fwd",
    )(*args)


def _merge_kernel(ya_ref, yb_ref, yc_ref, g0_ref, g1_ref, g2_ref, bm_ref, wa_ref, wb_ref, wc_ref, wo_ref,
                  h_ref, mod_ref, gp_ref, o_ref):
    bm = bm_ref[...]
    d = wo_ref.shape[0]
    nh = 2 if d % (2 * LANES) == 0 else 1
    dh = d // nh
    merged = []
    for c in range(nh):
        cs = slice(c * dh, (c + 1) * dh)
        gate = lambda g_ref, k: jax.nn.sigmoid(g_ref[0, :, cs].astype(F32) + bm[k:k + 1, cs])
        m = gate(g0_ref, 0) * jnp.dot(ya_ref[0], wa_ref[:, cs], preferred_element_type=F32)
        m += gate(g1_ref, 1) * jnp.dot(yb_ref[0], wb_ref[:, cs], preferred_element_type=F32)
        m += gate(g2_ref, 2) * jnp.dot(yc_ref[0], wc_ref[:, cs], preferred_element_type=F32)
        merged.append(m.astype(BF16))
    out = jnp.dot(merged[0], wo_ref[0:dh, :], preferred_element_type=F32)
    for c in range(1, nh):
        out += jnp.dot(merged[c], wo_ref[c * dh:(c + 1) * dh, :], preferred_element_type=F32)
    o_ref[0] = h_ref[0] + mod_ref[0][2:3] * _rms(out, gp_ref[...])


def _merge(ya, yb, yc, gates, b_merge, wba, wbb, wbc, wo, h, mods, g_post):
    b, s, d = h.shape
    tm = _tile(s, 256)
    per_batch = mods.shape[0] > 1
    row = lambda w: pl.BlockSpec((1, tm, w), lambda bi, i: (bi, i, 0))
    gate = lambda k: pl.BlockSpec((1, tm, d), lambda bi, i: (bi, i, k))
    full = lambda a: pl.BlockSpec(a.shape, lambda bi, i: (0, 0))
    return pl.pallas_call(
        _merge_kernel,
        grid=(b, s // tm),
        in_specs=[row(ya.shape[2]), row(yb.shape[2]), row(yc.shape[2]), gate(0), gate(1), gate(2),
                  full(b_merge), full(wba), full(wbb), full(wbc), full(wo), row(d),
                  pl.BlockSpec((1, N_MOD, d), (lambda bi, i: (bi, 0, 0)) if per_batch else (lambda bi, i: (0, 0, 0))),
                  pl.BlockSpec((1, d), lambda bi, i: (0, 0))],
        out_specs=row(d),
        out_shape=jax.ShapeDtypeStruct((b, s, d), F32),
        compiler_params=_params("arbitrary", "arbitrary"),
        name="merge",
    )(ya, yb, yc, gates, gates, gates, b_merge, wba, wbb, wbc, wo, h, mods, g_post)


def _ffn_kernel(h_ref, mod_ref, gpre_ref, gpost_ref, wg_ref, wu_ref, wd_ref, o_ref, un_ref, acc_ref):
    j = pl.program_id(2)

    @pl.when(j == 0)
    def _():
        m = mod_ref[0]
        un_ref[...] = (_rms(h_ref[0], gpre_ref[...]) * (1.0 + m[4:5]) + m[3:4]).astype(BF16)
        acc_ref[...] = jnp.zeros(acc_ref.shape, F32)

    u = un_ref[...]
    hg = jnp.dot(u, wg_ref[...], preferred_element_type=F32)
    hu = jnp.dot(u, wu_ref[...], preferred_element_type=F32)
    acc_ref[...] += jnp.dot((_silu(hg) * hu).astype(BF16), wd_ref[...], preferred_element_type=F32)

    @pl.when(j == pl.num_programs(2) - 1)
    def _():
        o_ref[0] = h_ref[0] + mod_ref[0][5:6] * _rms(acc_ref[...], gpost_ref[...])


def _ffn(h, mods, g_pre, g_post, wg, wu, wd):
    b, s, d = h.shape
    hidden = wg.shape[1]
    tm = _tile(s, 512)
    th = _tile(hidden, 512)
    per_batch = mods.shape[0] > 1
    return pl.pallas_call(
        _ffn_kernel,
        grid=(b, s // tm, hidden // th),
        in_specs=[pl.BlockSpec((1, tm, d), lambda bi, i, j: (bi, i, 0)),
                  pl.BlockSpec((1, N_MOD, d),
                               (lambda bi, i, j: (bi, 0, 0)) if per_batch else (lambda bi, i, j: (0, 0, 0))),
                  pl.BlockSpec((1, d), lambda bi, i, j: (0, 0)),
                  pl.BlockSpec((1, d), lambda bi, i, j: (0, 0)),
                  pl.BlockSpec((d, th), lambda bi, i, j: (0, j)),
                  pl.BlockSpec((d, th), lambda bi, i, j: (0, j)),
                  pl.BlockSpec((th, d), lambda bi, i, j: (j, 0))],
        out_specs=pl.BlockSpec((1, tm, d), lambda bi, i, j: (bi, i, 0)),
        out_shape=jax.ShapeDtypeStruct((b, s, d), F32),
        scratch_shapes=[pltpu.VMEM((tm, d), BF16), pltpu.VMEM((tm, d), F32)],
        compiler_params=_params("arbitrary", "arbitrary", "arbitrary"),
        name="ffn",
    )(h, mods, g_pre, g_post, wg, wu, wd)


def _rope_tables(n_tokens, hd):
    freqs = hd // 4
    pos = jnp.arange(n_tokens)
    rowcol = jnp.stack([(pos // GRID_W).astype(F32), (pos % GRID_W).astype(F32)], axis=1)
    inv = ROPE_BASE ** (-jnp.arange(freqs, dtype=F32) / freqs)
    lane = jnp.arange(LANES) % hd
    axis, half, f = lane // (2 * freqs), (lane % (2 * freqs)) // freqs, lane % freqs
    ang = rowcol[:, axis] * inv[f][None, :]
    cos, sin = jnp.cos(ang), jnp.sin(ang)
    return cos, jnp.where(half == 1, sin, 0.0), jnp.where(half == 0, -sin, 0.0)


def kernel(x, c, ctx, c_ctx, w_ada, b_ada, g_pre_mix, g_post_mix, g_pre_ffn, g_post_ffn, w_in, diff_lambda, diff_subln, conv_w, conv_b, rnn_conv_w, rnn_conv_b, rg_wa, rg_ba, rg_wx, rg_bx, rg_lambda, b_merge, w_branch_a, w_branch_b, w_branch_c, w_o, w_ffn_gate, w_ffn_up, w_ffn_down):
    b, s, d = x.shape
    depth = w_ada.shape[0]
    hd = diff_lambda.shape[-1]
    diff_w = w_branch_a.shape[1]
    conv_width = conv_w.shape[-1]
    rnn_width = rnn_conv_w.shape[-1]
    nblk, bw = rg_wa.shape[2], rg_wa.shape[3]
    assert diff_w == conv_width == rnn_width and 2 * hd == LANES
    gate_col_rnn = 3

    rows = -(-(b + 1) // SUBLANES) * SUBLANES
    cc = jnp.zeros((rows, d), F32).at[:b].set(c).at[b].set(c_ctx)
    mods = _ada(cc, w_ada, b_ada.reshape(depth, 1, N_MOD * d))
    tables = _rope_tables(s, hd)
    q_scale = hd ** -0.5 * math.log2(math.e)

    vec = lambda a: a.reshape(2, nblk, 1, bw)
    n_ctx = ctx.shape[1]
    flat = lambda a: a.reshape(1, b * n_ctx, a.shape[-1])
    per_batch = lambda a: a.reshape(b, n_ctx, a.shape[-1])
    h_lat, h_ctx = x, flat(ctx)
    for l in range(depth):
        need_ctx = l < depth - 1
        lam_init = 0.8 - 0.6 * math.exp(-0.3 * l)
        li = jnp.full((1, 1), lam_init, F32)
        ml = mods[l, :b].reshape(b, N_MOD, d)
        mc = mods[l, b:b + 1].reshape(1, N_MOD, d)
        w_in_l = w_in[l].astype(BF16)
        g_pre = g_pre_mix[l].reshape(1, d)
        subln = diff_subln[l].reshape(2 * hd, 1)

        qk_l, vt_l, rest_l, gates_l = _inproj(h_lat, ml, g_pre, w_in_l, tables, width=diff_w, q_scale=q_scale)
        qk_c, vt_c, rest_c, gates_c = _inproj(h_ctx, mc, g_pre, w_in_l, None, width=diff_w, q_scale=q_scale)
        qk_c, rest_c = per_batch(qk_c), per_batch(rest_c)

        ya_l = _attn(qk_l, qk_c, vt_c, qk_l, vt_l, diff_lambda[l], subln, li, hd=hd)

        prep = functools.partial(_prep, conv_w=conv_w[l], conv_b=conv_b[l].reshape(1, -1),
                                 rnn_conv_w=rnn_conv_w[l], rnn_conv_b=rnn_conv_b[l].reshape(1, -1),
                                 conv_width=conv_width, rnn_width=rnn_width)
        yb_l, xr_l = prep(rest_l)
        yb_c, xr_c = prep(rest_c)

        scan = functools.partial(_scan, wa=rg_wa[l], wx=rg_wx[l], ba=vec(rg_ba[l]), bx=vec(rg_bx[l]),
                                 lam=vec(rg_lambda[l]), gate_col=gate_col_rnn)
        zero = jnp.zeros((b, rnn_width), F32)
        hf_c, hfin_f = scan(xr_c, h0=zero, hf=None, rest=None, direction=0)
        yc_c, hfin_b = scan(xr_c, h0=zero, hf=hf_c, rest=rest_c, direction=1)
        hf_l, _ = scan(xr_l, h0=hfin_f, hf=None, rest=None, direction=0)
        yc_l, _ = scan(xr_l, h0=hfin_b, hf=hf_l, rest=rest_l, direction=1)

        wba, wbb, wbc = w_branch_a[l].astype(BF16), w_branch_b[l].astype(BF16), w_branch_c[l].astype(BF16)
        wo = w_o[l].astype(BF16)
        wg, wu, wd = w_ffn_gate[l].astype(BF16), w_ffn_up[l].astype(BF16), w_ffn_down[l].astype(BF16)
        g_post = g_post_mix[l].reshape(1, d)
        gf_pre, gf_post = g_pre_ffn[l].reshape(1, d), g_post_ffn[l].reshape(1, d)

        h_lat = _merge(ya_l, yb_l, yc_l, gates_l, b_merge[l], wba, wbb, wbc, wo, h_lat, ml, g_post)
        h_lat = _ffn(h_lat, ml, gf_pre, gf_post, wg, wu, wd)
        if need_ctx:
            ya_c = _attn(qk_c, qk_c, vt_c, None, None, diff_lambda[l], subln, li, hd=hd)
            h_ctx = _merge(flat(ya_c), flat(yb_c), flat(yc_c), gates_c, b_merge[l], wba, wbb, wbc, wo, h_ctx, mc,
                           g_post)
            h_ctx = _ffn(h_ctx, mc, gf_pre, gf_post, wg, wu, wd)
    return h_lat
```

```python
import functools
import math

import jax
import jax.numpy as jnp
from jax import lax
from jax.experimental import pallas as pl
from jax.experimental.pallas import tpu as pltpu

GRID_W = 64
ROPE_BASE = 10000.0
EPS = 1e-6
RG_C = 8.0
N_MOD = 6
N_BRANCH = 3
N_MID = 5
LANES = 128
SUBLANES = 8
VMEM_LIMIT_BYTES = 56 * 1024 * 1024
INPROJ_CHUNK = 256
ONES_ROWS = 16
SCAN_ROW_PAD = 8

F32 = jnp.float32
BF16 = jnp.bfloat16


def _tile(n, pref):
    if n <= pref:
        return n
    t = pref - pref % SUBLANES
    while t >= SUBLANES:
        if n % t == 0:
            return t
        t -= SUBLANES
    return n


def _params(*sem):
    return pltpu.CompilerParams(dimension_semantics=sem, vmem_limit_bytes=VMEM_LIMIT_BYTES)


def _rms(x, g):
    return x * lax.rsqrt(jnp.mean(x * x, axis=-1, keepdims=True) + EPS) * g


def _silu(x):
    return x * jax.nn.sigmoid(x)


def _ada_kernel(x_ref, w_ref, b_ref, o_ref):
    sx = _silu(x_ref[...]).astype(BF16)
    o_ref[0] = jnp.dot(sx, w_ref[0].astype(BF16), preferred_element_type=F32) + b_ref[0]


def _ada(cc, w_ada, b_flat):
    depth, d, n = w_ada.shape
    rows = cc.shape[0]
    tn = _tile(n, 1024)
    return pl.pallas_call(
        _ada_kernel,
        grid=(depth, n // tn),
        in_specs=[
            pl.BlockSpec((rows, d), lambda l, j: (0, 0)),
            pl.BlockSpec((1, d, tn), lambda l, j: (l, 0, j)),
            pl.BlockSpec((1, 1, tn), lambda l, j: (l, 0, j)),
        ],
        out_specs=pl.BlockSpec((1, rows, tn), lambda l, j: (l, 0, j)),
        out_shape=jax.ShapeDtypeStruct((depth, rows, n), F32),
        compiler_params=_params("arbitrary", "arbitrary"),
        name="ada",
    )(cc, w_ada, b_flat)


def _inproj_kernel(h_ref, mod_ref, g_ref, w_ref, *rest, rope, q_scale):
    if rope:
        cos_ref, s1_ref, s2_ref, qkv_ref, vt_ref, mid_ref, gates_ref, xn_ref = rest
    else:
        qkv_ref, vt_ref, mid_ref, gates_ref, xn_ref = rest
    j = pl.program_id(2)

    @pl.when(j == 0)
    def _():
        m = mod_ref[0]
        y = _rms(h_ref[0], g_ref[...])
        xn_ref[...] = (y * (1.0 + m[1:2]) + m[0:1]).astype(BF16)

    tn = w_ref.shape[1]
    cw = INPROJ_CHUNK if tn % INPROJ_CHUNK == 0 else tn

    def chunked(epilogue):
        for c in range(tn // cw):
            epilogue(c * cw, jnp.dot(xn_ref[...], w_ref[:, c * cw:(c + 1) * cw], preferred_element_type=F32))

    def qk_epilogue(c0, acc):
        a = acc * jnp.where(j == 0, q_scale, 1.0)
        if rope:
            cos, s1, s2 = cos_ref[...], s1_ref[...], s2_ref[...]
            quarter = LANES // 8
            for c in range(cw // LANES):
                blk = a[:, c * LANES:(c + 1) * LANES]
                r = blk * cos + pltpu.roll(blk, quarter, 1) * s1 + pltpu.roll(blk, LANES - quarter, 1) * s2
                qkv_ref[0, :, c0 + c * LANES:c0 + (c + 1) * LANES] = r.astype(BF16)
        else:
            qkv_ref[0, :, c0:c0 + cw] = a.astype(BF16)

    def v_epilogue(c0, acc):
        vt_ref[0, c0:c0 + cw, :] = acc.T.astype(BF16)

    def mid_epilogue(c0, acc):
        mid_ref[0, :, c0:c0 + cw] = acc.astype(BF16)

    def gates_epilogue(c0, acc):
        gates_ref[0, :, c0:c0 + cw] = acc.astype(BF16)

    pl.when(j < 2)(lambda: chunked(qk_epilogue))
    pl.when(j == 2)(lambda: chunked(v_epilogue))
    pl.when((j >= 3) & (j < 3 + N_MID))(lambda: chunked(mid_epilogue))
    pl.when(j >= 3 + N_MID)(lambda: chunked(gates_epilogue))


def _inproj(h, mods, g, w, tables, *, width, q_scale):
    b, s, d = h.shape
    n = w.shape[1]
    tn = width
    assert n == (3 + N_MID) * width + N_BRANCH * d and d % tn == 0
    tm = _tile(s, 1024)
    rope = tables is not None
    per_batch = mods.shape[0] > 1
    in_specs = [
        pl.BlockSpec((1, tm, d), lambda bi, i, j: (bi, i, 0)),
        pl.BlockSpec((1, N_MOD, d), (lambda bi, i, j: (bi, 0, 0)) if per_batch else (lambda bi, i, j: (0, 0, 0))),
        pl.BlockSpec((1, d), lambda bi, i, j: (0, 0)),
        pl.BlockSpec((d, tn), lambda bi, i, j: (0, j)),
    ]
    args = [h, mods, g, w]
    if rope:
        in_specs += [pl.BlockSpec((tm, LANES), lambda bi, i, j: (i, 0))] * 3
        args += list(tables)
    return pl.pallas_call(
        functools.partial(_inproj_kernel, rope=rope, q_scale=q_scale),
        grid=(b, s // tm, n // tn),
        in_specs=in_specs,
        out_specs=[
            pl.BlockSpec((1, tm, tn), lambda bi, i, j: (bi, i, jnp.minimum(j, 1))),
            pl.BlockSpec((1, tn, tm), lambda bi, i, j: (bi, 0, i)),
            pl.BlockSpec((1, tm, tn), lambda bi, i, j: (bi, i, jnp.clip(j - 3, 0, N_MID - 1))),
            pl.BlockSpec((1, tm, tn), lambda bi, i, j: (bi, i, jnp.maximum(j - 3 - N_MID, 0))),
        ],
        out_shape=[
            jax.ShapeDtypeStruct((b, s, 2 * width), BF16),
            jax.ShapeDtypeStruct((b, width, s), BF16),
            jax.ShapeDtypeStruct((b, s, N_MID * width), BF16),
            jax.ShapeDtypeStruct((b, s, N_BRANCH * d), BF16),
        ],
        scratch_shapes=[pltpu.VMEM((tm, d), BF16)],
        compiler_params=_params("arbitrary", "arbitrary", "arbitrary"),
        name="inproj_rope" if rope else "inproj",
    )(*args)


def _attn_kernel(*refs, heads, hd, has_lat):
    if has_lat:
        (q_ref, kc_ref, vc_ref, kl_ref, vl_ref, dl_ref, sg_ref, li_ref, o_ref,
         qs_ref, m_ref, acc_ref, s_ref) = refs
    else:
        q_ref, kc_ref, vc_ref, dl_ref, sg_ref, li_ref, o_ref, qs_ref, m_ref, acc_ref, s_ref = refs
    j = pl.program_id(2)
    last = pl.num_programs(2) - 1
    tq = q_ref.shape[1]
    hw = 2 * hd

    @pl.when(j == 0)
    def _():
        lane = lax.broadcasted_iota(jnp.int32, (tq, hw), 1)
        for h in range(heads):
            q = q_ref[0, :, h * hw:(h + 1) * hw].astype(F32)
            qs_ref[h, 0:tq] = jnp.where(lane < hd, q, 0.0).astype(BF16)
            qs_ref[h, tq:2 * tq] = jnp.where(lane >= hd, q, 0.0).astype(BF16)
        m_ref[...] = jnp.full(m_ref.shape, -jnp.inf, F32)
        acc_ref[...] = jnp.zeros(acc_ref.shape, F32)

    def step(k_ref, vt_ref):
        tk = k_ref.shape[1]
        ones = jnp.ones((ONES_ROWS, tk), BF16)

        def scores(h):
            k = k_ref[0, :, h * hw:(h + 1) * hw]
            s_ref[h % 2, 0:tk, :] = lax.dot_general(k, qs_ref[h], (((1,), (1,)), ((), ())),
                                                    preferred_element_type=F32)

        scores(0)
        for h in range(heads):
            if h + 1 < heads:
                scores(h + 1)
            vt = jnp.concatenate([vt_ref[0, h * hw:(h + 1) * hw, :], ones], axis=0)
            s = s_ref[h % 2, 0:tk, :]
            m_prev = m_ref[h]
            m_new = jnp.maximum(m_prev, jnp.max(s, axis=0, keepdims=True))
            alpha = jnp.exp2(m_prev - m_new)
            p = jnp.exp2(s - m_new)
            acc_ref[h] = alpha * acc_ref[h] + jnp.dot(vt, p.astype(BF16), preferred_element_type=F32)
            m_ref[h] = m_new

    if has_lat:
        @pl.when(j == 0)
        def _():
            step(kc_ref, vc_ref)

        @pl.when(j > 0)
        def _():
            step(kl_ref, vl_ref)
    else:
        step(kc_ref, vc_ref)

    @pl.when(j == last)
    def _():
        dl = dl_ref[...]
        lam_init = li_ref[...]
        lam = (jnp.exp(jnp.sum(dl[0:1] * dl[1:2], axis=-1, keepdims=True))
               - jnp.exp(jnp.sum(dl[2:3] * dl[3:4], axis=-1, keepdims=True)) + lam_init)
        for h in range(heads):
            acc = acc_ref[h, 0:hw]
            l = acc_ref[h, hw:hw + 1]
            o = acc[:, 0:tq] / l[:, 0:tq] - lam * (acc[:, tq:2 * tq] / l[:, tq:2 * tq])
            o = o * lax.rsqrt(jnp.mean(o * o, axis=0, keepdims=True) + EPS) * sg_ref[...] * (1.0 - lam_init)
            o_ref[0, :, h * hw:(h + 1) * hw] = o.T.astype(BF16)


def _attn(qk_q, qk_c, vt_c, qk_l, vt_l, diff_lambda, subln, lam_init, *, hd):
    b, sq, w2 = qk_q.shape
    width = w2 // 2
    heads = width // (2 * hd)
    n_ctx = qk_c.shape[1]
    has_lat = qk_l is not None
    tq = _tile(sq, 512)
    vc_map = (lambda bi, i, j: (bi, 0, 0)) if vt_c.shape[0] == b else (lambda bi, i, j: (0, 0, bi))
    in_specs = [
        pl.BlockSpec((1, tq, width), lambda bi, i, j: (bi, i, 0)),
        pl.BlockSpec((1, n_ctx, width), lambda bi, i, j: (bi, 0, 1)),
        pl.BlockSpec((1, width, n_ctx), vc_map),
    ]
    args = [qk_q, qk_c, vt_c]
    nkv = 1
    tk = n_ctx
    if has_lat:
        sk = qk_l.shape[1]
        tk = _tile(sk, 1024)
        assert tk >= n_ctx
        nkv += sk // tk
        in_specs += [
            pl.BlockSpec((1, tk, width), lambda bi, i, j: (bi, jnp.maximum(j - 1, 0), 1)),
            pl.BlockSpec((1, width, tk), lambda bi, i, j: (bi, 0, jnp.maximum(j - 1, 0))),
        ]
        args += [qk_l, vt_l]
    in_specs += [
        pl.BlockSpec(diff_lambda.shape, lambda bi, i, j: (0, 0)),
        pl.BlockSpec((2 * hd, 1), lambda bi, i, j: (0, 0)),
        pl.BlockSpec((1, 1), lambda bi, i, j: (0, 0)),
    ]
    args += [diff_lambda, subln, lam_init]
    return pl.pallas_call(
        functools.partial(_attn_kernel, heads=heads, hd=hd, has_lat=has_lat),
        grid=(b, sq // tq, nkv),
        in_specs=in_specs,
        out_specs=pl.BlockSpec((1, tq, width), lambda bi, i, j: (bi, i, 0)),
        out_shape=jax.ShapeDtypeStruct((b, sq, width), BF16),
        scratch_shapes=[
            pltpu.VMEM((heads, 2 * tq, 2 * hd), BF16),
            pltpu.VMEM((heads, 1, 2 * tq), F32),
            pltpu.VMEM((heads, 2 * hd + ONES_ROWS, 2 * tq), F32),
            pltpu.VMEM((2, tk, 2 * tq), F32),
        ],
        compiler_params=_params("arbitrary", "arbitrary", "arbitrary"),
        name="attn_lat" if has_lat else "attn_ctx",
    )(*args)


def _prep_kernel(x3_ref, x4_ref, x5_ref, x7_ref, cw_ref, cb_ref, rw_ref, rb_ref, yb_ref, xr_ref):
    s = x3_ref.shape[1]
    row = lax.broadcasted_iota(jnp.int32, (s, x3_ref.shape[2]), 0)

    def shifted(x, k):
        r = pltpu.roll(x, k % s, 0)
        return jnp.where((row >= k) & (row < s + k), r, 0.0)

    z = x4_ref[0].astype(F32) * x3_ref[0].astype(F32)
    cw = cw_ref[...]
    conv = cw[0:1] * shifted(z, 1) + cw[1:2] * z + cw[2:3] * shifted(z, -1) + cb_ref[...]
    yb_ref[0] = (x5_ref[0].astype(F32) * conv).astype(BF16)
    x = x7_ref[0].astype(F32)
    rw = rw_ref[...]
    xr_ref[0] = (rw[0:1] * shifted(x, 2) + rw[1:2] * shifted(x, 1) + rw[2:3] * x + rw[3:4] * shifted(x, -1)
                 + rb_ref[...])


def _prep(rest, conv_w, conv_b, rnn_conv_w, rnn_conv_b, *, conv_width, rnn_width):
    b, s, _ = rest.shape
    assert conv_width == rnn_width
    tc = LANES
    nct = conv_width // tc
    col = lambda k: (lambda bi, c: (bi, 0, k * nct + c))
    par = lambda rows: pl.BlockSpec((rows, tc), lambda bi, c: (0, c))
    return pl.pallas_call(
        _prep_kernel,
        grid=(b, nct),
        in_specs=[pl.BlockSpec((1, s, tc), col(0)), pl.BlockSpec((1, s, tc), col(1)),
                  pl.BlockSpec((1, s, tc), col(2)), pl.BlockSpec((1, s, tc), col(4)),
                  par(conv_w.shape[0]), par(1), par(rnn_conv_w.shape[0]), par(1)],
        out_specs=[pl.BlockSpec((1, s, tc), lambda bi, c: (bi, 0, c)),
                   pl.BlockSpec((1, s, tc), lambda bi, c: (bi, 0, c))],
        out_shape=[jax.ShapeDtypeStruct((b, s, conv_width), BF16),
                   jax.ShapeDtypeStruct((b, s, rnn_width), F32)],
        compiler_params=_params("arbitrary", "arbitrary"),
        name="conv_prep",
    )(rest, rest, rest, rest, conv_w, conv_b, rnn_conv_w, rnn_conv_b)


def _scan_kernel(*refs, reverse, finalize):
    if finalize:
        (xr_ref, wa_ref, wx_ref, ba_ref, bx_ref, lam_ref, h0_ref, hf_ref, gate_ref, out_ref, hlast_ref,
         a_s, g_s, h_s, carry) = refs
    else:
        xr_ref, wa_ref, wx_ref, ba_ref, bx_ref, lam_ref, h0_ref, out_ref, hlast_ref, a_s, g_s, h_s, carry = refs
    i = pl.program_id(1)
    nb, tc, cw = xr_ref.shape

    @pl.when(i == 0)
    def _():
        carry[...] = h0_ref[...]

    x = xr_ref[...].reshape(nb * tc, cw)
    xb = x.astype(BF16)
    r = jax.nn.sigmoid(jnp.dot(xb, wa_ref[0, 0].astype(BF16), preferred_element_type=F32) + ba_ref[0, 0])
    gi = jax.nn.sigmoid(jnp.dot(xb, wx_ref[0, 0].astype(BF16), preferred_element_type=F32) + bx_ref[0, 0])
    z = -lam_ref[0, 0]
    softplus = jnp.maximum(z, 0.0) + jnp.log1p(jnp.exp(-jnp.abs(z)))
    a = jnp.exp(-RG_C * r * softplus)
    g = jnp.sqrt(1.0 - a * a) * (gi * x)
    pitch = tc + SCAN_ROW_PAD
    for bi in range(nb):
        a_s[bi * pitch:bi * pitch + tc, :] = a[bi * tc:(bi + 1) * tc]
        g_s[bi * pitch:bi * pitch + tc, :] = g[bi * tc:(bi + 1) * tc]

    def body(t, h):
        tt = tc - 1 - t if reverse else t
        rows = pl.ds(tt, nb, stride=pitch)
        h = a_s[rows, :] * h + g_s[rows, :]
        h_s[rows, :] = h
        return h

    h = lax.fori_loop(0, tc, body, carry[...], unroll=8)
    carry[...] = h
    hlast_ref[...] = h
    for bi in range(nb):
        hs = h_s[bi * pitch:bi * pitch + tc, :]
        if finalize:
            out_ref[bi] = (jax.nn.gelu(gate_ref[bi].astype(F32)) * (hf_ref[bi] + hs)).astype(out_ref.dtype)
        else:
            out_ref[bi] = hs


def _scan(xr, wa, wx, ba, bx, lam, h0, hf, rest, *, direction, gate_col):
    b, s, c = xr.shape
    nblk, bw = wa.shape[1], wa.shape[2]
    assert bw == LANES and nblk * bw == c
    tc = _tile(s, 512)
    nchunk = s // tc
    reverse = direction == 1
    finalize = hf is not None
    chunk = (lambda i: nchunk - 1 - i) if reverse else (lambda i: i)
    d = direction
    seq_spec = pl.BlockSpec((b, tc, bw), lambda n, i: (0, chunk(i), n))
    w_spec = pl.BlockSpec((1, 1, bw, bw), lambda n, i: (d, n, 0, 0))
    v_spec = pl.BlockSpec((1, 1, 1, bw), lambda n, i: (d, n, 0, 0))
    st_spec = pl.BlockSpec((b, bw), lambda n, i: (0, n))
    in_specs = [seq_spec, w_spec, w_spec, v_spec, v_spec, v_spec, st_spec]
    args = [xr, wa, wx, ba, bx, lam, h0]
    if finalize:
        in_specs += [seq_spec, pl.BlockSpec((b, tc, bw), lambda n, i: (0, chunk(i), gate_col * nblk + n))]
        args += [hf, rest]
    return pl.pallas_call(
        functools.partial(_scan_kernel, reverse=reverse, finalize=finalize),
        grid=(nblk, nchunk),
        in_specs=in_specs,
        out_specs=[seq_spec, st_spec],
        out_shape=[jax.ShapeDtypeStruct((b, s, c), BF16 if finalize else F32),
                   jax.ShapeDtypeStruct((b, c), F32)],
        scratch_shapes=[pltpu.VMEM((b * (tc + SCAN_ROW_PAD), bw), F32)] * 3 + [pltpu.VMEM((b, bw), F32)],
        compiler_params=_params("arbitrary", "arbitrary"),
        name="rglru_bwd" if reverse else "rglru_fwd",
    )(*args)


def _merge_kernel(ya_ref, yb_ref, yc_ref, g0_ref, g1_ref, g2_ref, bm_ref, wa_ref, wb_ref, wc_ref, wo_ref,
                  h_ref, mod_ref, gp_ref, o_ref):
    bm = bm_ref[...]
    d = wo_ref.shape[0]
    nh = 2 if d % (2 * LANES) == 0 else 1
    dh = d // nh
    merged = []
    for c in range(nh):
        cs = slice(c * dh, (c + 1) * dh)
        gate = lambda g_ref, k: jax.nn.sigmoid(g_ref[0, :, cs].astype(F32) + bm[k:k + 1, cs])
        m = gate(g0_ref, 0) * jnp.dot(ya_ref[0], wa_ref[:, cs], preferred_element_type=F32)
        m += gate(g1_ref, 1) * jnp.dot(yb_ref[0], wb_ref[:, cs], preferred_element_type=F32)
        m += gate(g2_ref, 2) * jnp.dot(yc_ref[0], wc_ref[:, cs], preferred_element_type=F32)
        merged.append(m.astype(BF16))
    out = jnp.dot(merged[0], wo_ref[0:dh, :], preferred_element_type=F32)
    for c in range(1, nh):
        out += jnp.dot(merged[c], wo_ref[c * dh:(c + 1) * dh, :], preferred_element_type=F32)
    o_ref[0] = h_ref[0] + mod_ref[0][2:3] * _rms(out, gp_ref[...])


def _merge(ya, yb, yc, gates, b_merge, wba, wbb, wbc, wo, h, mods, g_post):
    b, s, d = h.shape
    tm = _tile(s, 256)
    per_batch = mods.shape[0] > 1
    row = lambda w: pl.BlockSpec((1, tm, w), lambda bi, i: (bi, i, 0))
    gate = lambda k: pl.BlockSpec((1, tm, d), lambda bi, i: (bi, i, k))
    full = lambda a: pl.BlockSpec(a.shape, lambda bi, i: (0, 0))
    return pl.pallas_call(
        _merge_kernel,
        grid=(b, s // tm),
        in_specs=[row(ya.shape[2]), row(yb.shape[2]), row(yc.shape[2]), gate(0), gate(1), gate(2),
                  full(b_merge), full(wba), full(wbb), full(wbc), full(wo), row(d),
                  pl.BlockSpec((1, N_MOD, d), (lambda bi, i: (bi, 0, 0)) if per_batch else (lambda bi, i: (0, 0, 0))),
                  pl.BlockSpec((1, d), lambda bi, i: (0, 0))],
        out_specs=row(d),
        out_shape=jax.ShapeDtypeStruct((b, s, d), F32),
        compiler_params=_params("arbitrary", "arbitrary"),
        name="merge",
    )(ya, yb, yc, gates, gates, gates, b_merge, wba, wbb, wbc, wo, h, mods, g_post)


def _ffn_kernel(h_ref, mod_ref, gpre_ref, gpost_ref, wg_ref, wu_ref, wd_ref, o_ref, un_ref, acc_ref, act_ref):
    j = pl.program_id(2)
    nh = pl.num_programs(2) - 1

    def gate_up():
        u = un_ref[...]
        hg = jnp.dot(u, wg_ref[...], preferred_element_type=F32)
        hu = jnp.dot(u, wu_ref[...], preferred_element_type=F32)
        return (_silu(hg) * hu).astype(BF16)

    def down():
        return jnp.dot(act_ref[(j + 1) % 2], wd_ref[...], preferred_element_type=F32)

    @pl.when(j == 0)
    def _():
        m = mod_ref[0]
        un_ref[...] = (_rms(h_ref[0], gpre_ref[...]) * (1.0 + m[4:5]) + m[3:4]).astype(BF16)
        act_ref[0] = gate_up()

    @pl.when(j == 1)
    def _():
        act = gate_up()
        acc_ref[...] = down()
        act_ref[1] = act

    @pl.when((j > 1) & (j < nh))
    def _():
        act = gate_up()
        acc_ref[...] += down()
        act_ref[j % 2] = act

    @pl.when(j == nh)
    def _():
        o_ref[0] = h_ref[0] + mod_ref[0][5:6] * _rms(acc_ref[...] + down(), gpost_ref[...])


def _ffn(h, mods, g_pre, g_post, wg, wu, wd):
    b, s, d = h.shape
    hidden = wg.shape[1]
    tm = _tile(s, 512)
    th = _tile(hidden, 512)
    nt = hidden // th
    assert nt >= 2
    per_batch = mods.shape[0] > 1
    return pl.pallas_call(
        _ffn_kernel,
        grid=(b, s // tm, nt + 1),
        in_specs=[pl.BlockSpec((1, tm, d), lambda bi, i, j: (bi, i, 0)),
                  pl.BlockSpec((1, N_MOD, d),
                               (lambda bi, i, j: (bi, 0, 0)) if per_batch else (lambda bi, i, j: (0, 0, 0))),
                  pl.BlockSpec((1, d), lambda bi, i, j: (0, 0)),
                  pl.BlockSpec((1, d), lambda bi, i, j: (0, 0)),
                  pl.BlockSpec((d, th), lambda bi, i, j: (0, jnp.minimum(j, nt - 1))),
                  pl.BlockSpec((d, th), lambda bi, i, j: (0, jnp.minimum(j, nt - 1))),
                  pl.BlockSpec((th, d), lambda bi, i, j: (jnp.maximum(j - 1, 0), 0))],
        out_specs=pl.BlockSpec((1, tm, d), lambda bi, i, j: (bi, i, 0)),
        out_shape=jax.ShapeDtypeStruct((b, s, d), F32),
        scratch_shapes=[pltpu.VMEM((tm, d), BF16), pltpu.VMEM((tm, d), F32), pltpu.VMEM((2, tm, th), BF16)],
        compiler_params=_params("arbitrary", "arbitrary", "arbitrary"),
        name="ffn",
    )(h, mods, g_pre, g_post, wg, wu, wd)


def _rope_tables(n_tokens, hd):
    freqs = hd // 4
    pos = jnp.arange(n_tokens)
    rowcol = jnp.stack([(pos // GRID_W).astype(F32), (pos % GRID_W).astype(F32)], axis=1)
    inv = ROPE_BASE ** (-jnp.arange(freqs, dtype=F32) / freqs)
    lane = jnp.arange(LANES) % hd
    axis, half, f = lane // (2 * freqs), (lane % (2 * freqs)) // freqs, lane % freqs
    ang = rowcol[:, axis] * inv[f][None, :]
    cos, sin = jnp.cos(ang), jnp.sin(ang)
    return cos, jnp.where(half == 1, sin, 0.0), jnp.where(half == 0, -sin, 0.0)


def kernel(x, c, ctx, c_ctx, w_ada, b_ada, g_pre_mix, g_post_mix, g_pre_ffn, g_post_ffn, w_in, diff_lambda, diff_subln, conv_w, conv_b, rnn_conv_w, rnn_conv_b, rg_wa, rg_ba, rg_wx, rg_bx, rg_lambda, b_merge, w_branch_a, w_branch_b, w_branch_c, w_o, w_ffn_gate, w_ffn_up, w_ffn_down):
    b, s, d = x.shape
    depth = w_ada.shape[0]
    hd = diff_lambda.shape[-1]
    diff_w = w_branch_a.shape[1]
    conv_width = conv_w.shape[-1]
    rnn_width = rnn_conv_w.shape[-1]
    nblk, bw = rg_wa.shape[2], rg_wa.shape[3]
    assert diff_w == conv_width == rnn_width and 2 * hd == LANES
    gate_col_rnn = 3

    rows = -(-(b + 1) // SUBLANES) * SUBLANES
    cc = jnp.zeros((rows, d), F32).at[:b].set(c).at[b].set(c_ctx)
    mods = _ada(cc, w_ada, b_ada.reshape(depth, 1, N_MOD * d))
    tables = _rope_tables(s, hd)
    q_scale = hd ** -0.5 * math.log2(math.e)

    vec = lambda a: a.reshape(2, nblk, 1, bw)
    n_ctx = ctx.shape[1]
    flat = lambda a: a.reshape(1, b * n_ctx, a.shape[-1])
    per_batch = lambda a: a.reshape(b, n_ctx, a.shape[-1])
    h_lat, h_ctx = x, flat(ctx)
    for l in range(depth):
        need_ctx = l < depth - 1
        lam_init = 0.8 - 0.6 * math.exp(-0.3 * l)
        li = jnp.full((1, 1), lam_init, F32)
        ml = mods[l, :b].reshape(b, N_MOD, d)
        mc = mods[l, b:b + 1].reshape(1, N_MOD, d)
        w_in_l = w_in[l].astype(BF16)
        g_pre = g_pre_mix[l].reshape(1, d)
        subln = diff_subln[l].reshape(2 * hd, 1)

        qk_l, vt_l, rest_l, gates_l = _inproj(h_lat, ml, g_pre, w_in_l, tables, width=diff_w, q_scale=q_scale)
        qk_c, vt_c, rest_c, gates_c = _inproj(h_ctx, mc, g_pre, w_in_l, None, width=diff_w, q_scale=q_scale)
        qk_c, rest_c = per_batch(qk_c), per_batch(rest_c)

        ya_l = _attn(qk_l, qk_c, vt_c, qk_l, vt_l, diff_lambda[l], subln, li, hd=hd)

        prep = functools.partial(_prep, conv_w=conv_w[l], conv_b=conv_b[l].reshape(1, -1),
                                 rnn_conv_w=rnn_conv_w[l], rnn_conv_b=rnn_conv_b[l].reshape(1, -1),
                                 conv_width=conv_width, rnn_width=rnn_width)
        yb_l, xr_l = prep(rest_l)
        yb_c, xr_c = prep(rest_c)

        scan = functools.partial(_scan, wa=rg_wa[l], wx=rg_wx[l], ba=vec(rg_ba[l]), bx=vec(rg_bx[l]),
                                 lam=vec(rg_lambda[l]), gate_col=gate_col_rnn)
        zero = jnp.zeros((b, rnn_width), F32)
        hf_c, hfin_f = scan(xr_c, h0=zero, hf=None, rest=None, direction=0)
        yc_c, hfin_b = scan(xr_c, h0=zero, hf=hf_c, rest=rest_c, direction=1)
        hf_l, _ = scan(xr_l, h0=hfin_f, hf=None, rest=None, direction=0)
        yc_l, _ = scan(xr_l, h0=hfin_b, hf=hf_l, rest=rest_l, direction=1)

        wba, wbb, wbc = w_branch_a[l].astype(BF16), w_branch_b[l].astype(BF16), w_branch_c[l].astype(BF16)
        wo = w_o[l].astype(BF16)
        wg, wu, wd = w_ffn_gate[l].astype(BF16), w_ffn_up[l].astype(BF16), w_ffn_down[l].astype(BF16)
        g_post = g_post_mix[l].reshape(1, d)
        gf_pre, gf_post = g_pre_ffn[l].reshape(1, d), g_post_ffn[l].reshape(1, d)

        h_lat = _merge(ya_l, yb_l, yc_l, gates_l, b_merge[l], wba, wbb, wbc, wo, h_lat, ml, g_post)
        h_lat = _ffn(h_lat, ml, gf_pre, gf_post, wg, wu, wd)
        if need_ctx:
            ya_c = _attn(qk_c, qk_c, vt_c, None, None, diff_lambda[l], subln, li, hd=hd)
            h_ctx = _merge(flat(ya_c), flat(yb_c), flat(yc_c), gates_c, b_merge[l], wba, wbb, wbc, wo, h_ctx, mc,
                           g_post)
            h_ctx = _ffn(h_ctx, mc, gf_pre, gf_post, wg, wu, wd)
    return h_lat
```

```python
import functools
import math

import jax
import jax.numpy as jnp
from jax import lax
from jax.experimental import pallas as pl
from jax.experimental.pallas import tpu as pltpu

GRID_W = 64
ROPE_BASE = 10000.0
EPS = 1e-6
RG_C = 8.0
N_MOD = 6
N_BRANCH = 3
N_MID = 5
LANES = 128
SUBLANES = 8
VMEM_LIMIT_BYTES = 56 * 1024 * 1024
INPROJ_CHUNK = 256
ONES_ROWS = 16
SCAN_ROW_PAD = 8

F32 = jnp.float32
BF16 = jnp.bfloat16


def _tile(n, pref):
    if n <= pref:
        return n
    t = pref - pref % SUBLANES
    while t >= SUBLANES:
        if n % t == 0:
            return t
        t -= SUBLANES
    return n


def _params(*sem):
    return pltpu.CompilerParams(dimension_semantics=sem, vmem_limit_bytes=VMEM_LIMIT_BYTES)


def _rms(x, g):
    return x * lax.rsqrt(jnp.mean(x * x, axis=-1, keepdims=True) + EPS) * g


def _silu(x):
    return x * jax.nn.sigmoid(x)


def _ada_kernel(x_ref, w_ref, b_ref, o_ref):
    sx = _silu(x_ref[...]).astype(BF16)
    o_ref[0] = jnp.dot(sx, w_ref[0].astype(BF16), preferred_element_type=F32) + b_ref[0]


def _ada(cc, w_ada, b_flat):
    depth, d, n = w_ada.shape
    rows = cc.shape[0]
    tn = _tile(n, 1024)
    return pl.pallas_call(
        _ada_kernel,
        grid=(depth, n // tn),
        in_specs=[
            pl.BlockSpec((rows, d), lambda l, j: (0, 0)),
            pl.BlockSpec((1, d, tn), lambda l, j: (l, 0, j)),
            pl.BlockSpec((1, 1, tn), lambda l, j: (l, 0, j)),
        ],
        out_specs=pl.BlockSpec((1, rows, tn), lambda l, j: (l, 0, j)),
        out_shape=jax.ShapeDtypeStruct((depth, rows, n), F32),
        compiler_params=_params("arbitrary", "arbitrary"),
        name="ada",
    )(cc, w_ada, b_flat)


def _inproj_kernel(h_ref, mod_ref, g_ref, w_ref, *rest, rope, q_scale):
    if rope:
        cos_ref, s1_ref, s2_ref, qkv_ref, vt_ref, mid_ref, gates_ref, xn_ref = rest
    else:
        qkv_ref, vt_ref, mid_ref, gates_ref, xn_ref = rest
    j = pl.program_id(2)

    @pl.when(j == 0)
    def _():
        m = mod_ref[0]
        y = _rms(h_ref[0], g_ref[...])
        xn_ref[...] = (y * (1.0 + m[1:2]) + m[0:1]).astype(BF16)

    tn = w_ref.shape[2]
    cw = INPROJ_CHUNK if tn % INPROJ_CHUNK == 0 else tn

    def chunked(epilogue):
        for c in range(tn // cw):
            epilogue(c * cw, jnp.dot(xn_ref[...], w_ref[0, :, c * cw:(c + 1) * cw], preferred_element_type=F32))

    def qk_epilogue(c0, acc):
        a = acc * jnp.where(j == 0, q_scale, 1.0)
        if rope:
            cos, s1, s2 = cos_ref[...], s1_ref[...], s2_ref[...]
            quarter = LANES // 8
            for c in range(cw // LANES):
                blk = a[:, c * LANES:(c + 1) * LANES]
                r = blk * cos + pltpu.roll(blk, quarter, 1) * s1 + pltpu.roll(blk, LANES - quarter, 1) * s2
                qkv_ref[0, :, c0 + c * LANES:c0 + (c + 1) * LANES] = r.astype(BF16)
        else:
            qkv_ref[0, :, c0:c0 + cw] = a.astype(BF16)

    def v_epilogue(c0, acc):
        vt_ref[0, c0:c0 + cw, :] = acc.T.astype(BF16)

    def mid_epilogue(c0, acc):
        mid_ref[0, :, c0:c0 + cw] = acc.astype(BF16)

    def gates_epilogue(c0, acc):
        gates_ref[0, :, c0:c0 + cw] = acc.astype(BF16)

    pl.when(j < 2)(lambda: chunked(qk_epilogue))
    pl.when(j == 2)(lambda: chunked(v_epilogue))
    pl.when((j >= 3) & (j < 3 + N_MID))(lambda: chunked(mid_epilogue))
    pl.when(j >= 3 + N_MID)(lambda: chunked(gates_epilogue))


def _inproj(h, mods, g, w, tables, *, width, q_scale):
    b, s, d = h.shape
    tn = width
    n = w.shape[0] * tn
    assert w.shape[2] == tn and n == (3 + N_MID) * width + N_BRANCH * d and d % tn == 0
    tm = _tile(s, 1024)
    rope = tables is not None
    per_batch = mods.shape[0] > 1
    in_specs = [
        pl.BlockSpec((1, tm, d), lambda bi, i, j: (bi, i, 0)),
        pl.BlockSpec((1, N_MOD, d), (lambda bi, i, j: (bi, 0, 0)) if per_batch else (lambda bi, i, j: (0, 0, 0))),
        pl.BlockSpec((1, d), lambda bi, i, j: (0, 0)),
        pl.BlockSpec((1, d, tn), lambda bi, i, j: (j, 0, 0)),
    ]
    args = [h, mods, g, w]
    if rope:
        in_specs += [pl.BlockSpec((tm, LANES), lambda bi, i, j: (i, 0))] * 3
        args += list(tables)
    return pl.pallas_call(
        functools.partial(_inproj_kernel, rope=rope, q_scale=q_scale),
        grid=(b, s // tm, n // tn),
        in_specs=in_specs,
        out_specs=[
            pl.BlockSpec((1, tm, tn), lambda bi, i, j: (bi, i, jnp.minimum(j, 1))),
            pl.BlockSpec((1, tn, tm), lambda bi, i, j: (bi, 0, i)),
            pl.BlockSpec((1, tm, tn), lambda bi, i, j: (bi, i, jnp.clip(j - 3, 0, N_MID - 1))),
            pl.BlockSpec((1, tm, tn), lambda bi, i, j: (bi, i, jnp.maximum(j - 3 - N_MID, 0))),
        ],
        out_shape=[
            jax.ShapeDtypeStruct((b, s, 2 * width), BF16),
            jax.ShapeDtypeStruct((b, width, s), BF16),
            jax.ShapeDtypeStruct((b, s, N_MID * width), BF16),
            jax.ShapeDtypeStruct((b, s, N_BRANCH * d), BF16),
        ],
        scratch_shapes=[pltpu.VMEM((tm, d), BF16)],
        compiler_params=_params("arbitrary", "arbitrary", "arbitrary"),
        name="inproj_rope" if rope else "inproj",
    )(*args)


def _attn_kernel(*refs, heads, hd, has_lat):
    if has_lat:
        (q_ref, kc_ref, vc_ref, kl_ref, vl_ref, dl_ref, sg_ref, li_ref, o_ref,
         qs_ref, m_ref, acc_ref, s_ref) = refs
    else:
        q_ref, kc_ref, vc_ref, dl_ref, sg_ref, li_ref, o_ref, qs_ref, m_ref, acc_ref, s_ref = refs
    j = pl.program_id(2)
    last = pl.num_programs(2) - 1
    tq = q_ref.shape[1]
    hw = 2 * hd

    @pl.when(j == 0)
    def _():
        lane = lax.broadcasted_iota(jnp.int32, (tq, hw), 1)
        for h in range(heads):
            q = q_ref[0, :, h * hw:(h + 1) * hw].astype(F32)
            qs_ref[h, 0:tq] = jnp.where(lane < hd, q, 0.0).astype(BF16)
            qs_ref[h, tq:2 * tq] = jnp.where(lane >= hd, q, 0.0).astype(BF16)
        m_ref[...] = jnp.full(m_ref.shape, -jnp.inf, F32)
        acc_ref[...] = jnp.zeros(acc_ref.shape, F32)

    def step(k_ref, vt_ref):
        tk = k_ref.shape[1]
        ones = jnp.ones((ONES_ROWS, tk), BF16)

        def scores(h):
            k = k_ref[0, :, h * hw:(h + 1) * hw]
            s_ref[h % 2, 0:tk, :] = lax.dot_general(k, qs_ref[h], (((1,), (1,)), ((), ())),
                                                    preferred_element_type=F32)

        scores(0)
        for h in range(heads):
            if h + 1 < heads:
                scores(h + 1)
            vt = jnp.concatenate([vt_ref[0, h * hw:(h + 1) * hw, :], ones], axis=0)
            s = s_ref[h % 2, 0:tk, :]
            m_prev = m_ref[h]
            m_new = jnp.maximum(m_prev, jnp.max(s, axis=0, keepdims=True))
            alpha = jnp.exp2(m_prev - m_new)
            p = jnp.exp2(s - m_new)
            acc_ref[h] = alpha * acc_ref[h] + jnp.dot(vt, p.astype(BF16), preferred_element_type=F32)
            m_ref[h] = m_new

    if has_lat:
        @pl.when(j == 0)
        def _():
            step(kc_ref, vc_ref)

        @pl.when(j > 0)
        def _():
            step(kl_ref, vl_ref)
    else:
        step(kc_ref, vc_ref)

    @pl.when(j == last)
    def _():
        dl = dl_ref[...]
        lam_init = li_ref[...]
        lam = (jnp.exp(jnp.sum(dl[0:1] * dl[1:2], axis=-1, keepdims=True))
               - jnp.exp(jnp.sum(dl[2:3] * dl[3:4], axis=-1, keepdims=True)) + lam_init)
        for h in range(heads):
            acc = acc_ref[h, 0:hw]
            l = acc_ref[h, hw:hw + 1]
            o = acc[:, 0:tq] / l[:, 0:tq] - lam * (acc[:, tq:2 * tq] / l[:, tq:2 * tq])
            o = o * lax.rsqrt(jnp.mean(o * o, axis=0, keepdims=True) + EPS) * sg_ref[...] * (1.0 - lam_init)
            o_ref[0, :, h * hw:(h + 1) * hw] = o.T.astype(BF16)


def _attn(qk_q, qk_c, vt_c, qk_l, vt_l, diff_lambda, subln, lam_init, *, hd):
    b, sq, w2 = qk_q.shape
    width = w2 // 2
    heads = width // (2 * hd)
    n_ctx = qk_c.shape[1]
    has_lat = qk_l is not None
    tq = _tile(sq, 512)
    vc_map = (lambda bi, i, j: (bi, 0, 0)) if vt_c.shape[0] == b else (lambda bi, i, j: (0, 0, bi))
    in_specs = [
        pl.BlockSpec((1, tq, width), lambda bi, i, j: (bi, i, 0)),
        pl.BlockSpec((1, n_ctx, width), lambda bi, i, j: (bi, 0, 1)),
        pl.BlockSpec((1, width, n_ctx), vc_map),
    ]
    args = [qk_q, qk_c, vt_c]
    nkv = 1
    tk = n_ctx
    if has_lat:
        sk = qk_l.shape[1]
        tk = _tile(sk, 1024)
        assert tk >= n_ctx
        nkv += sk // tk
        in_specs += [
            pl.BlockSpec((1, tk, width), lambda bi, i, j: (bi, jnp.maximum(j - 1, 0), 1)),
            pl.BlockSpec((1, width, tk), lambda bi, i, j: (bi, 0, jnp.maximum(j - 1, 0))),
        ]
        args += [qk_l, vt_l]
    in_specs += [
        pl.BlockSpec(diff_lambda.shape, lambda bi, i, j: (0, 0)),
        pl.BlockSpec((2 * hd, 1), lambda bi, i, j: (0, 0)),
        pl.BlockSpec((1, 1), lambda bi, i, j: (0, 0)),
    ]
    args += [diff_lambda, subln, lam_init]
    return pl.pallas_call(
        functools.partial(_attn_kernel, heads=heads, hd=hd, has_lat=has_lat),
        grid=(b, sq // tq, nkv),
        in_specs=in_specs,
        out_specs=pl.BlockSpec((1, tq, width), lambda bi, i, j: (bi, i, 0)),
        out_shape=jax.ShapeDtypeStruct((b, sq, width), BF16),
        scratch_shapes=[
            pltpu.VMEM((heads, 2 * tq, 2 * hd), BF16),
            pltpu.VMEM((heads, 1, 2 * tq), F32),
            pltpu.VMEM((heads, 2 * hd + ONES_ROWS, 2 * tq), F32),
            pltpu.VMEM((2, tk, 2 * tq), F32),
        ],
        compiler_params=_params("arbitrary", "arbitrary", "arbitrary"),
        name="attn_lat" if has_lat else "attn_ctx",
    )(*args)


def _prep_kernel(x3_ref, x4_ref, x5_ref, x7_ref, cw_ref, cb_ref, rw_ref, rb_ref, yb_ref, xr_ref):
    s = x3_ref.shape[1]
    row = lax.broadcasted_iota(jnp.int32, (s, x3_ref.shape[2]), 0)

    def shifted(x, k):
        r = pltpu.roll(x, k % s, 0)
        return jnp.where((row >= k) & (row < s + k), r, 0.0)

    z = x4_ref[0].astype(F32) * x3_ref[0].astype(F32)
    cw = cw_ref[...]
    conv = cw[0:1] * shifted(z, 1) + cw[1:2] * z + cw[2:3] * shifted(z, -1) + cb_ref[...]
    yb_ref[0] = (x5_ref[0].astype(F32) * conv).astype(BF16)
    x = x7_ref[0].astype(F32)
    rw = rw_ref[...]
    xr_ref[0] = (rw[0:1] * shifted(x, 2) + rw[1:2] * shifted(x, 1) + rw[2:3] * x + rw[3:4] * shifted(x, -1)
                 + rb_ref[...])


def _prep(rest, conv_w, conv_b, rnn_conv_w, rnn_conv_b, *, conv_width, rnn_width):
    b, s, _ = rest.shape
    assert conv_width == rnn_width
    tc = LANES
    nct = conv_width // tc
    col = lambda k: (lambda bi, c: (bi, 0, k * nct + c))
    par = lambda rows: pl.BlockSpec((rows, tc), lambda bi, c: (0, c))
    return pl.pallas_call(
        _prep_kernel,
        grid=(b, nct),
        in_specs=[pl.BlockSpec((1, s, tc), col(0)), pl.BlockSpec((1, s, tc), col(1)),
                  pl.BlockSpec((1, s, tc), col(2)), pl.BlockSpec((1, s, tc), col(4)),
                  par(conv_w.shape[0]), par(1), par(rnn_conv_w.shape[0]), par(1)],
        out_specs=[pl.BlockSpec((1, s, tc), lambda bi, c: (bi, 0, c)),
                   pl.BlockSpec((1, s, tc), lambda bi, c: (bi, 0, c))],
        out_shape=[jax.ShapeDtypeStruct((b, s, conv_width), BF16),
                   jax.ShapeDtypeStruct((b, s, rnn_width), F32)],
        compiler_params=_params("arbitrary", "arbitrary"),
        name="conv_prep",
    )(rest, rest, rest, rest, conv_w, conv_b, rnn_conv_w, rnn_conv_b)


def _scan_kernel(*refs, reverse, finalize):
    if finalize:
        (xr_ref, wa_ref, wx_ref, ba_ref, bx_ref, lam_ref, h0_ref, hf_ref, gate_ref, out_ref, hlast_ref,
         a_s, g_s, h_s, carry) = refs
    else:
        xr_ref, wa_ref, wx_ref, ba_ref, bx_ref, lam_ref, h0_ref, out_ref, hlast_ref, a_s, g_s, h_s, carry = refs
    i = pl.program_id(1)
    nb, tc, cw = xr_ref.shape

    @pl.when(i == 0)
    def _():
        carry[...] = h0_ref[...]

    x = xr_ref[...].reshape(nb * tc, cw)
    xb = x.astype(BF16)
    r = jax.nn.sigmoid(jnp.dot(xb, wa_ref[0, 0].astype(BF16), preferred_element_type=F32) + ba_ref[0, 0])
    gi = jax.nn.sigmoid(jnp.dot(xb, wx_ref[0, 0].astype(BF16), preferred_element_type=F32) + bx_ref[0, 0])
    z = -lam_ref[0, 0]
    softplus = jnp.maximum(z, 0.0) + jnp.log1p(jnp.exp(-jnp.abs(z)))
    a = jnp.exp(-RG_C * r * softplus)
    g = jnp.sqrt(1.0 - a * a) * (gi * x)
    pitch = tc + SCAN_ROW_PAD
    for bi in range(nb):
        a_s[bi * pitch:bi * pitch + tc, :] = a[bi * tc:(bi + 1) * tc]
        g_s[bi * pitch:bi * pitch + tc, :] = g[bi * tc:(bi + 1) * tc]

    def body(t, h):
        tt = tc - 1 - t if reverse else t
        rows = pl.ds(tt, nb, stride=pitch)
        h = a_s[rows, :] * h + g_s[rows, :]
        h_s[rows, :] = h
        return h

    h = lax.fori_loop(0, tc, body, carry[...], unroll=8)
    carry[...] = h
    hlast_ref[...] = h
    for bi in range(nb):
        hs = h_s[bi * pitch:bi * pitch + tc, :]
        if finalize:
            out_ref[bi] = (jax.nn.gelu(gate_ref[bi].astype(F32)) * (hf_ref[bi] + hs)).astype(out_ref.dtype)
        else:
            out_ref[bi] = hs


def _scan(xr, wa, wx, ba, bx, lam, h0, hf, rest, *, direction, gate_col):
    b, s, c = xr.shape
    nblk, bw = wa.shape[1], wa.shape[2]
    assert bw == LANES and nblk * bw == c
    tc = _tile(s, 512)
    nchunk = s // tc
    reverse = direction == 1
    finalize = hf is not None
    chunk = (lambda i: nchunk - 1 - i) if reverse else (lambda i: i)
    d = direction
    seq_spec = pl.BlockSpec((b, tc, bw), lambda n, i: (0, chunk(i), n))
    w_spec = pl.BlockSpec((1, 1, bw, bw), lambda n, i: (d, n, 0, 0))
    v_spec = pl.BlockSpec((1, 1, 1, bw), lambda n, i: (d, n, 0, 0))
    st_spec = pl.BlockSpec((b, bw), lambda n, i: (0, n))
    in_specs = [seq_spec, w_spec, w_spec, v_spec, v_spec, v_spec, st_spec]
    args = [xr, wa, wx, ba, bx, lam, h0]
    if finalize:
        in_specs += [seq_spec, pl.BlockSpec((b, tc, bw), lambda n, i: (0, chunk(i), gate_col * nblk + n))]
        args += [hf, rest]
    return pl.pallas_call(
        functools.partial(_scan_kernel, reverse=reverse, finalize=finalize),
        grid=(nblk, nchunk),
        in_specs=in_specs,
        out_specs=[seq_spec, st_spec],
        out_shape=[jax.ShapeDtypeStruct((b, s, c), BF16 if finalize else F32),
                   jax.ShapeDtypeStruct((b, c), F32)],
        scratch_shapes=[pltpu.VMEM((b * (tc + SCAN_ROW_PAD), bw), F32)] * 3 + [pltpu.VMEM((b, bw), F32)],
        compiler_params=_params("arbitrary", "arbitrary"),
        name="rglru_bwd" if reverse else "rglru_fwd",
    )(*args)


def _merge_kernel(ya_ref, yb_ref, yc_ref, g0_ref, g1_ref, g2_ref, bm_ref, wa_ref, wb_ref, wc_ref, wo_ref,
                  h_ref, mod_ref, gp_ref, o_ref):
    bm = bm_ref[...]
    d = wo_ref.shape[0]
    nh = 2 if d % (2 * LANES) == 0 else 1
    dh = d // nh
    merged = []
    for c in range(nh):
        cs = slice(c * dh, (c + 1) * dh)
        gate = lambda g_ref, k: jax.nn.sigmoid(g_ref[0, :, cs].astype(F32) + bm[k:k + 1, cs])
        m = gate(g0_ref, 0) * jnp.dot(ya_ref[0], wa_ref[:, cs], preferred_element_type=F32)
        m += gate(g1_ref, 1) * jnp.dot(yb_ref[0], wb_ref[:, cs], preferred_element_type=F32)
        m += gate(g2_ref, 2) * jnp.dot(yc_ref[0], wc_ref[:, cs], preferred_element_type=F32)
        merged.append(m.astype(BF16))
    out = jnp.dot(merged[0], wo_ref[0:dh, :], preferred_element_type=F32)
    for c in range(1, nh):
        out += jnp.dot(merged[c], wo_ref[c * dh:(c + 1) * dh, :], preferred_element_type=F32)
    o_ref[0] = h_ref[0] + mod_ref[0][2:3] * _rms(out, gp_ref[...])


def _merge(ya, yb, yc, gates, b_merge, wba, wbb, wbc, wo, h, mods, g_post):
    b, s, d = h.shape
    tm = _tile(s, 256)
    per_batch = mods.shape[0] > 1
    row = lambda w: pl.BlockSpec((1, tm, w), lambda bi, i: (bi, i, 0))
    gate = lambda k: pl.BlockSpec((1, tm, d), lambda bi, i: (bi, i, k))
    full = lambda a: pl.BlockSpec(a.shape, lambda bi, i: (0, 0))
    return pl.pallas_call(
        _merge_kernel,
        grid=(b, s // tm),
        in_specs=[row(ya.shape[2]), row(yb.shape[2]), row(yc.shape[2]), gate(0), gate(1), gate(2),
                  full(b_merge), full(wba), full(wbb), full(wbc), full(wo), row(d),
                  pl.BlockSpec((1, N_MOD, d), (lambda bi, i: (bi, 0, 0)) if per_batch else (lambda bi, i: (0, 0, 0))),
                  pl.BlockSpec((1, d), lambda bi, i: (0, 0))],
        out_specs=row(d),
        out_shape=jax.ShapeDtypeStruct((b, s, d), F32),
        compiler_params=_params("arbitrary", "arbitrary"),
        name="merge",
    )(ya, yb, yc, gates, gates, gates, b_merge, wba, wbb, wbc, wo, h, mods, g_post)


def _ffn_kernel(h_ref, mod_ref, gpre_ref, gpost_ref, wg_ref, wu_ref, wd_ref, o_ref, un_ref, acc_ref, act_ref):
    j = pl.program_id(2)
    nh = pl.num_programs(2) - 1

    def gate_up():
        u = un_ref[...]
        hg = jnp.dot(u, wg_ref[0], preferred_element_type=F32)
        hu = jnp.dot(u, wu_ref[0], preferred_element_type=F32)
        return (_silu(hg) * hu).astype(BF16)

    def down():
        return jnp.dot(act_ref[(j + 1) % 2], wd_ref[...], preferred_element_type=F32)

    @pl.when(j == 0)
    def _():
        m = mod_ref[0]
        un_ref[...] = (_rms(h_ref[0], gpre_ref[...]) * (1.0 + m[4:5]) + m[3:4]).astype(BF16)
        act_ref[0] = gate_up()

    @pl.when(j == 1)
    def _():
        act = gate_up()
        acc_ref[...] = down()
        act_ref[1] = act

    @pl.when((j > 1) & (j < nh))
    def _():
        act = gate_up()
        acc_ref[...] += down()
        act_ref[j % 2] = act

    @pl.when(j == nh)
    def _():
        o_ref[0] = h_ref[0] + mod_ref[0][5:6] * _rms(acc_ref[...] + down(), gpost_ref[...])


def _ffn(h, mods, g_pre, g_post, wg, wu, wd):
    b, s, d = h.shape
    nt, _, th = wg.shape
    tm = _tile(s, 512)
    assert nt >= 2
    per_batch = mods.shape[0] > 1
    return pl.pallas_call(
        _ffn_kernel,
        grid=(b, s // tm, nt + 1),
        in_specs=[pl.BlockSpec((1, tm, d), lambda bi, i, j: (bi, i, 0)),
                  pl.BlockSpec((1, N_MOD, d),
                               (lambda bi, i, j: (bi, 0, 0)) if per_batch else (lambda bi, i, j: (0, 0, 0))),
                  pl.BlockSpec((1, d), lambda bi, i, j: (0, 0)),
                  pl.BlockSpec((1, d), lambda bi, i, j: (0, 0)),
                  pl.BlockSpec((1, d, th), lambda bi, i, j: (jnp.minimum(j, nt - 1), 0, 0)),
                  pl.BlockSpec((1, d, th), lambda bi, i, j: (jnp.minimum(j, nt - 1), 0, 0)),
                  pl.BlockSpec((th, d), lambda bi, i, j: (jnp.maximum(j - 1, 0), 0))],
        out_specs=pl.BlockSpec((1, tm, d), lambda bi, i, j: (bi, i, 0)),
        out_shape=jax.ShapeDtypeStruct((b, s, d), F32),
        scratch_shapes=[pltpu.VMEM((tm, d), BF16), pltpu.VMEM((tm, d), F32), pltpu.VMEM((2, tm, th), BF16)],
        compiler_params=_params("arbitrary", "arbitrary", "arbitrary"),
        name="ffn",
    )(h, mods, g_pre, g_post, wg, wu, wd)


def _rope_tables(n_tokens, hd):
    freqs = hd // 4
    pos = jnp.arange(n_tokens)
    rowcol = jnp.stack([(pos // GRID_W).astype(F32), (pos % GRID_W).astype(F32)], axis=1)
    inv = ROPE_BASE ** (-jnp.arange(freqs, dtype=F32) / freqs)
    lane = jnp.arange(LANES) % hd
    axis, half, f = lane // (2 * freqs), (lane % (2 * freqs)) // freqs, lane % freqs
    ang = rowcol[:, axis] * inv[f][None, :]
    cos, sin = jnp.cos(ang), jnp.sin(ang)
    return cos, jnp.where(half == 1, sin, 0.0), jnp.where(half == 0, -sin, 0.0)


def kernel(x, c, ctx, c_ctx, w_ada, b_ada, g_pre_mix, g_post_mix, g_pre_ffn, g_post_ffn, w_in, diff_lambda, diff_subln, conv_w, conv_b, rnn_conv_w, rnn_conv_b, rg_wa, rg_ba, rg_wx, rg_bx, rg_lambda, b_merge, w_branch_a, w_branch_b, w_branch_c, w_o, w_ffn_gate, w_ffn_up, w_ffn_down):
    b, s, d = x.shape
    depth = w_ada.shape[0]
    hd = diff_lambda.shape[-1]
    diff_w = w_branch_a.shape[1]
    conv_width = conv_w.shape[-1]
    rnn_width = rnn_conv_w.shape[-1]
    nblk, bw = rg_wa.shape[2], rg_wa.shape[3]
    assert diff_w == conv_width == rnn_width and 2 * hd == LANES
    gate_col_rnn = 3

    rows = -(-(b + 1) // SUBLANES) * SUBLANES
    cc = jnp.zeros((rows, d), F32).at[:b].set(c).at[b].set(c_ctx)
    mods = _ada(cc, w_ada, b_ada.reshape(depth, 1, N_MOD * d))
    tables = _rope_tables(s, hd)
    q_scale = hd ** -0.5 * math.log2(math.e)

    vec = lambda a: a.reshape(2, nblk, 1, bw)
    col_tiles = lambda w, t: w.astype(BF16).reshape(w.shape[0], w.shape[1] // t, t).transpose(1, 0, 2)
    ffn_th = _tile(w_ffn_gate.shape[-1], 512)
    n_ctx = ctx.shape[1]
    flat = lambda a: a.reshape(1, b * n_ctx, a.shape[-1])
    per_batch = lambda a: a.reshape(b, n_ctx, a.shape[-1])
    h_lat, h_ctx = x, flat(ctx)
    for l in range(depth):
        need_ctx = l < depth - 1
        lam_init = 0.8 - 0.6 * math.exp(-0.3 * l)
        li = jnp.full((1, 1), lam_init, F32)
        ml = mods[l, :b].reshape(b, N_MOD, d)
        mc = mods[l, b:b + 1].reshape(1, N_MOD, d)
        w_in_l = col_tiles(w_in[l], diff_w)
        g_pre = g_pre_mix[l].reshape(1, d)
        subln = diff_subln[l].reshape(2 * hd, 1)

        qk_l, vt_l, rest_l, gates_l = _inproj(h_lat, ml, g_pre, w_in_l, tables, width=diff_w, q_scale=q_scale)
        qk_c, vt_c, rest_c, gates_c = _inproj(h_ctx, mc, g_pre, w_in_l, None, width=diff_w, q_scale=q_scale)
        qk_c, rest_c = per_batch(qk_c), per_batch(rest_c)

        ya_l = _attn(qk_l, qk_c, vt_c, qk_l, vt_l, diff_lambda[l], subln, li, hd=hd)

        prep = functools.partial(_prep, conv_w=conv_w[l], conv_b=conv_b[l].reshape(1, -1),
                                 rnn_conv_w=rnn_conv_w[l], rnn_conv_b=rnn_conv_b[l].reshape(1, -1),
                                 conv_width=conv_width, rnn_width=rnn_width)
        yb_l, xr_l = prep(rest_l)
        yb_c, xr_c = prep(rest_c)

        scan = functools.partial(_scan, wa=rg_wa[l], wx=rg_wx[l], ba=vec(rg_ba[l]), bx=vec(rg_bx[l]),
                                 lam=vec(rg_lambda[l]), gate_col=gate_col_rnn)
        zero = jnp.zeros((b, rnn_width), F32)
        hf_c, hfin_f = scan(xr_c, h0=zero, hf=None, rest=None, direction=0)
        yc_c, hfin_b = scan(xr_c, h0=zero, hf=hf_c, rest=rest_c, direction=1)
        hf_l, _ = scan(xr_l, h0=hfin_f, hf=None, rest=None, direction=0)
        yc_l, _ = scan(xr_l, h0=hfin_b, hf=hf_l, rest=rest_l, direction=1)

        wba, wbb, wbc = w_branch_a[l].astype(BF16), w_branch_b[l].astype(BF16), w_branch_c[l].astype(BF16)
        wo = w_o[l].astype(BF16)
        wg, wu, wd = col_tiles(w_ffn_gate[l], ffn_th), col_tiles(w_ffn_up[l], ffn_th), w_ffn_down[l].astype(BF16)
        g_post = g_post_mix[l].reshape(1, d)
        gf_pre, gf_post = g_pre_ffn[l].reshape(1, d), g_post_ffn[l].reshape(1, d)

        h_lat = _merge(ya_l, yb_l, yc_l, gates_l, b_merge[l], wba, wbb, wbc, wo, h_lat, ml, g_post)
        h_lat = _ffn(h_lat, ml, gf_pre, gf_post, wg, wu, wd)
        if need_ctx:
            ya_c = _attn(qk_c, qk_c, vt_c, None, None, diff_lambda[l], subln, li, hd=hd)
            h_ctx = _merge(flat(ya_c), flat(yb_c), flat(yc_c), gates_c, b_merge[l], wba, wbb, wbc, wo, h_ctx, mc,
                           g_post)
            h_ctx = _ffn(h_ctx, mc, gf_pre, gf_post, wg, wu, wd)
    return h_lat
```

```python
import functools
import math

import jax
import jax.numpy as jnp
from jax import lax
from jax.experimental import pallas as pl
from jax.experimental.pallas import tpu as pltpu

GRID_W = 64
ROPE_BASE = 10000.0
EPS = 1e-6
RG_C = 8.0
N_MOD = 6
N_BRANCH = 3
N_MID = 5
LANES = 128
SUBLANES = 8
VMEM_LIMIT_BYTES = 56 * 1024 * 1024
INPROJ_CHUNK = 512
ONES_ROWS = 16
SCAN_ROW_PAD = 8

F32 = jnp.float32
BF16 = jnp.bfloat16


def _tile(n, pref):
    if n <= pref:
        return n
    t = pref - pref % SUBLANES
    while t >= SUBLANES:
        if n % t == 0:
            return t
        t -= SUBLANES
    return n


def _params(*sem):
    return pltpu.CompilerParams(dimension_semantics=sem, vmem_limit_bytes=VMEM_LIMIT_BYTES)


def _rms(x, g):
    return x * lax.rsqrt(jnp.mean(x * x, axis=-1, keepdims=True) + EPS) * g


def _silu(x):
    return x * jax.nn.sigmoid(x)


def _ada_kernel(x_ref, w_ref, b_ref, o_ref):
    sx = _silu(x_ref[...]).astype(BF16)
    o_ref[0] = jnp.dot(sx, w_ref[0].astype(BF16), preferred_element_type=F32) + b_ref[0]


def _ada(cc, w_ada, b_flat):
    depth, d, n = w_ada.shape
    rows = cc.shape[0]
    tn = _tile(n, 1024)
    return pl.pallas_call(
        _ada_kernel,
        grid=(depth, n // tn),
        in_specs=[
            pl.BlockSpec((rows, d), lambda l, j: (0, 0)),
            pl.BlockSpec((1, d, tn), lambda l, j: (l, 0, j)),
            pl.BlockSpec((1, 1, tn), lambda l, j: (l, 0, j)),
        ],
        out_specs=pl.BlockSpec((1, rows, tn), lambda l, j: (l, 0, j)),
        out_shape=jax.ShapeDtypeStruct((depth, rows, n), F32),
        compiler_params=_params("arbitrary", "arbitrary"),
        name="ada",
    )(cc, w_ada, b_flat)


def _inproj_kernel(h_ref, mod_ref, g_ref, w_ref, *rest, rope, q_scale):
    if rope:
        cos_ref, s1_ref, s2_ref, qkv_ref, vt_ref, mid_ref, gates_ref, xn_ref = rest
    else:
        qkv_ref, vt_ref, mid_ref, gates_ref, xn_ref = rest
    j = pl.program_id(2)

    @pl.when(j == 0)
    def _():
        m = mod_ref[0]
        y = _rms(h_ref[0], g_ref[...])
        xn_ref[...] = (y * (1.0 + m[1:2]) + m[0:1]).astype(BF16)

    tn = w_ref.shape[2]
    cw = INPROJ_CHUNK if tn % INPROJ_CHUNK == 0 else tn

    def chunked(epilogue):
        for c in range(tn // cw):
            epilogue(c * cw, jnp.dot(xn_ref[...], w_ref[0, :, c * cw:(c + 1) * cw], preferred_element_type=F32))

    def qk_epilogue(c0, acc):
        a = acc * jnp.where(j == 0, q_scale, 1.0)
        if rope:
            cos, s1, s2 = cos_ref[...], s1_ref[...], s2_ref[...]
            quarter = LANES // 8
            for c in range(cw // LANES):
                blk = a[:, c * LANES:(c + 1) * LANES]
                r = blk * cos + pltpu.roll(blk, quarter, 1) * s1 + pltpu.roll(blk, LANES - quarter, 1) * s2
                qkv_ref[0, :, c0 + c * LANES:c0 + (c + 1) * LANES] = r.astype(BF16)
        else:
            qkv_ref[0, :, c0:c0 + cw] = a.astype(BF16)

    def v_epilogue(c0, acc):
        vt_ref[0, c0:c0 + cw, :] = acc.T.astype(BF16)

    def mid_epilogue(c0, acc):
        mid_ref[0, :, c0:c0 + cw] = acc.astype(BF16)

    def gates_epilogue(c0, acc):
        gates_ref[0, :, c0:c0 + cw] = acc.astype(BF16)

    pl.when(j < 2)(lambda: chunked(qk_epilogue))
    pl.when(j == 2)(lambda: chunked(v_epilogue))
    pl.when((j >= 3) & (j < 3 + N_MID))(lambda: chunked(mid_epilogue))
    pl.when(j >= 3 + N_MID)(lambda: chunked(gates_epilogue))


def _inproj(h, mods, g, w, layer, tables, *, width, q_scale):
    b, s, d = h.shape
    tn = width
    n = w.shape[2]
    assert n == (3 + N_MID) * width + N_BRANCH * d and d % tn == 0
    tm = _tile(s, 1024)
    rope = tables is not None
    per_batch = mods.shape[0] > 1
    in_specs = [
        pl.BlockSpec((1, tm, d), lambda bi, i, j: (bi, i, 0)),
        pl.BlockSpec((1, N_MOD, d), (lambda bi, i, j: (bi, 0, 0)) if per_batch else (lambda bi, i, j: (0, 0, 0))),
        pl.BlockSpec((1, d), lambda bi, i, j: (0, 0)),
        pl.BlockSpec((1, d, tn), lambda bi, i, j: (layer, 0, j)),
    ]
    args = [h, mods, g, w]
    if rope:
        in_specs += [pl.BlockSpec((tm, LANES), lambda bi, i, j: (i, 0))] * 3
        args += list(tables)
    return pl.pallas_call(
        functools.partial(_inproj_kernel, rope=rope, q_scale=q_scale),
        grid=(b, s // tm, n // tn),
        in_specs=in_specs,
        out_specs=[
            pl.BlockSpec((1, tm, tn), lambda bi, i, j: (bi, i, jnp.minimum(j, 1))),
            pl.BlockSpec((1, tn, tm), lambda bi, i, j: (bi, 0, i)),
            pl.BlockSpec((1, tm, tn), lambda bi, i, j: (bi, i, jnp.clip(j - 3, 0, N_MID - 1))),
            pl.BlockSpec((1, tm, tn), lambda bi, i, j: (bi, i, jnp.maximum(j - 3 - N_MID, 0))),
        ],
        out_shape=[
            jax.ShapeDtypeStruct((b, s, 2 * width), BF16),
            jax.ShapeDtypeStruct((b, width, s), BF16),
            jax.ShapeDtypeStruct((b, s, N_MID * width), BF16),
            jax.ShapeDtypeStruct((b, s, N_BRANCH * d), BF16),
        ],
        scratch_shapes=[pltpu.VMEM((tm, d), BF16)],
        compiler_params=_params("arbitrary", "arbitrary", "arbitrary"),
        name="inproj_rope" if rope else "inproj",
    )(*args)


def _attn_kernel(*refs, heads, hd, has_lat):
    if has_lat:
        (q_ref, kc_ref, vc_ref, kl_ref, vl_ref, dl_ref, sg_ref, li_ref, o_ref,
         qs_ref, m_ref, acc_ref, s_ref) = refs
    else:
        q_ref, kc_ref, vc_ref, dl_ref, sg_ref, li_ref, o_ref, qs_ref, m_ref, acc_ref, s_ref = refs
    j = pl.program_id(2)
    last = pl.num_programs(2) - 1
    tq = q_ref.shape[1]
    hw = 2 * hd

    @pl.when(j == 0)
    def _():
        lane = lax.broadcasted_iota(jnp.int32, (tq, hw), 1)
        for h in range(heads):
            q = q_ref[0, :, h * hw:(h + 1) * hw].astype(F32)
            qs_ref[h, 0:tq] = jnp.where(lane < hd, q, 0.0).astype(BF16)
            qs_ref[h, tq:2 * tq] = jnp.where(lane >= hd, q, 0.0).astype(BF16)
        m_ref[...] = jnp.full(m_ref.shape, -jnp.inf, F32)
        acc_ref[...] = jnp.zeros(acc_ref.shape, F32)

    def step(k_ref, vt_ref):
        tk = k_ref.shape[1]
        ones = jnp.ones((ONES_ROWS, tk), BF16)

        def scores(h):
            k = k_ref[0, :, h * hw:(h + 1) * hw]
            s_ref[h % 2, 0:tk, :] = lax.dot_general(k, qs_ref[h], (((1,), (1,)), ((), ())),
                                                    preferred_element_type=F32)

        scores(0)
        for h in range(heads):
            if h + 1 < heads:
                scores(h + 1)
            vt = jnp.concatenate([vt_ref[0, h * hw:(h + 1) * hw, :], ones], axis=0)
            s = s_ref[h % 2, 0:tk, :]
            m_prev = m_ref[h]
            m_new = jnp.maximum(m_prev, jnp.max(s, axis=0, keepdims=True))
            alpha = jnp.exp2(m_prev - m_new)
            p = jnp.exp2(s - m_new)
            acc_ref[h] = alpha * acc_ref[h] + jnp.dot(vt, p.astype(BF16), preferred_element_type=F32)
            m_ref[h] = m_new

    if has_lat:
        @pl.when(j == 0)
        def _():
            step(kc_ref, vc_ref)

        @pl.when(j > 0)
        def _():
            step(kl_ref, vl_ref)
    else:
        step(kc_ref, vc_ref)

    @pl.when(j == last)
    def _():
        dl = dl_ref[...]
        lam_init = li_ref[...]
        lam = (jnp.exp(jnp.sum(dl[0:1] * dl[1:2], axis=-1, keepdims=True))
               - jnp.exp(jnp.sum(dl[2:3] * dl[3:4], axis=-1, keepdims=True)) + lam_init)
        for h in range(heads):
            acc = acc_ref[h, 0:hw]
            l = acc_ref[h, hw:hw + 1]
            o = acc[:, 0:tq] / l[:, 0:tq] - lam * (acc[:, tq:2 * tq] / l[:, tq:2 * tq])
            o = o * lax.rsqrt(jnp.mean(o * o, axis=0, keepdims=True) + EPS) * sg_ref[...] * (1.0 - lam_init)
            o_ref[0, :, h * hw:(h + 1) * hw] = o.T.astype(BF16)


def _attn(qk_q, qk_c, vt_c, qk_l, vt_l, diff_lambda, subln, lam_init, *, hd):
    b, sq, w2 = qk_q.shape
    width = w2 // 2
    heads = width // (2 * hd)
    n_ctx = qk_c.shape[1]
    has_lat = qk_l is not None
    tq = _tile(sq, 512)
    vc_map = (lambda bi, i, j: (bi, 0, 0)) if vt_c.shape[0] == b else (lambda bi, i, j: (0, 0, bi))
    in_specs = [
        pl.BlockSpec((1, tq, width), lambda bi, i, j: (bi, i, 0)),
        pl.BlockSpec((1, n_ctx, width), lambda bi, i, j: (bi, 0, 1)),
        pl.BlockSpec((1, width, n_ctx), vc_map),
    ]
    args = [qk_q, qk_c, vt_c]
    nkv = 1
    tk = n_ctx
    if has_lat:
        sk = qk_l.shape[1]
        tk = _tile(sk, 1024)
        assert tk >= n_ctx
        nkv += sk // tk
        in_specs += [
            pl.BlockSpec((1, tk, width), lambda bi, i, j: (bi, jnp.maximum(j - 1, 0), 1)),
            pl.BlockSpec((1, width, tk), lambda bi, i, j: (bi, 0, jnp.maximum(j - 1, 0))),
        ]
        args += [qk_l, vt_l]
    in_specs += [
        pl.BlockSpec(diff_lambda.shape, lambda bi, i, j: (0, 0)),
        pl.BlockSpec((2 * hd, 1), lambda bi, i, j: (0, 0)),
        pl.BlockSpec((1, 1), lambda bi, i, j: (0, 0)),
    ]
    args += [diff_lambda, subln, lam_init]
    return pl.pallas_call(
        functools.partial(_attn_kernel, heads=heads, hd=hd, has_lat=has_lat),
        grid=(b, sq // tq, nkv),
        in_specs=in_specs,
        out_specs=pl.BlockSpec((1, tq, width), lambda bi, i, j: (bi, i, 0)),
        out_shape=jax.ShapeDtypeStruct((b, sq, width), BF16),
        scratch_shapes=[
            pltpu.VMEM((heads, 2 * tq, 2 * hd), BF16),
            pltpu.VMEM((heads, 1, 2 * tq), F32),
            pltpu.VMEM((heads, 2 * hd + ONES_ROWS, 2 * tq), F32),
            pltpu.VMEM((2, tk, 2 * tq), F32),
        ],
        compiler_params=_params("arbitrary", "arbitrary", "arbitrary"),
        name="attn_lat" if has_lat else "attn_ctx",
    )(*args)


def _prep_kernel(x3_ref, x4_ref, x5_ref, x7_ref, cw_ref, cb_ref, rw_ref, rb_ref, yb_ref, xr_ref):
    s = x3_ref.shape[1]
    row = lax.broadcasted_iota(jnp.int32, (s, x3_ref.shape[2]), 0)

    def shifted(x, k):
        r = pltpu.roll(x, k % s, 0)
        return jnp.where((row >= k) & (row < s + k), r, 0.0)

    z = x4_ref[0].astype(F32) * x3_ref[0].astype(F32)
    cw = cw_ref[...]
    conv = cw[0:1] * shifted(z, 1) + cw[1:2] * z + cw[2:3] * shifted(z, -1) + cb_ref[...]
    yb_ref[0] = (x5_ref[0].astype(F32) * conv).astype(BF16)
    x = x7_ref[0].astype(F32)
    rw = rw_ref[...]
    xr_ref[0] = (rw[0:1] * shifted(x, 2) + rw[1:2] * shifted(x, 1) + rw[2:3] * x + rw[3:4] * shifted(x, -1)
                 + rb_ref[...])


def _prep(rest, conv_w, conv_b, rnn_conv_w, rnn_conv_b, *, conv_width, rnn_width):
    b, s, _ = rest.shape
    assert conv_width == rnn_width
    tc = LANES
    nct = conv_width // tc
    col = lambda k: (lambda bi, c: (bi, 0, k * nct + c))
    par = lambda rows: pl.BlockSpec((rows, tc), lambda bi, c: (0, c))
    return pl.pallas_call(
        _prep_kernel,
        grid=(b, nct),
        in_specs=[pl.BlockSpec((1, s, tc), col(0)), pl.BlockSpec((1, s, tc), col(1)),
                  pl.BlockSpec((1, s, tc), col(2)), pl.BlockSpec((1, s, tc), col(4)),
                  par(conv_w.shape[0]), par(1), par(rnn_conv_w.shape[0]), par(1)],
        out_specs=[pl.BlockSpec((1, s, tc), lambda bi, c: (bi, 0, c)),
                   pl.BlockSpec((1, s, tc), lambda bi, c: (bi, 0, c))],
        out_shape=[jax.ShapeDtypeStruct((b, s, conv_width), BF16),
                   jax.ShapeDtypeStruct((b, s, rnn_width), F32)],
        compiler_params=_params("arbitrary", "arbitrary"),
        name="conv_prep",
    )(rest, rest, rest, rest, conv_w, conv_b, rnn_conv_w, rnn_conv_b)


def _scan_kernel(*refs, reverse, finalize):
    if finalize:
        (xr_ref, wa_ref, wx_ref, ba_ref, bx_ref, lam_ref, h0_ref, hf_ref, gate_ref, out_ref, hlast_ref,
         a_s, g_s, h_s, carry) = refs
    else:
        xr_ref, wa_ref, wx_ref, ba_ref, bx_ref, lam_ref, h0_ref, out_ref, hlast_ref, a_s, g_s, h_s, carry = refs
    i = pl.program_id(1)
    nb, tc, cw = xr_ref.shape

    @pl.when(i == 0)
    def _():
        carry[...] = h0_ref[...]

    x = xr_ref[...].reshape(nb * tc, cw)
    xb = x.astype(BF16)
    r = jax.nn.sigmoid(jnp.dot(xb, wa_ref[0, 0].astype(BF16), preferred_element_type=F32) + ba_ref[0, 0])
    gi = jax.nn.sigmoid(jnp.dot(xb, wx_ref[0, 0].astype(BF16), preferred_element_type=F32) + bx_ref[0, 0])
    z = -lam_ref[0, 0]
    softplus = jnp.maximum(z, 0.0) + jnp.log1p(jnp.exp(-jnp.abs(z)))
    a = jnp.exp(-RG_C * r * softplus)
    g = jnp.sqrt(1.0 - a * a) * (gi * x)
    pitch = tc + SCAN_ROW_PAD
    for bi in range(nb):
        a_s[bi * pitch:bi * pitch + tc, :] = a[bi * tc:(bi + 1) * tc]
        g_s[bi * pitch:bi * pitch + tc, :] = g[bi * tc:(bi + 1) * tc]

    def body(t, h):
        tt = tc - 1 - t if reverse else t
        rows = pl.ds(tt, nb, stride=pitch)
        h = a_s[rows, :] * h + g_s[rows, :]
        h_s[rows, :] = h
        return h

    h = lax.fori_loop(0, tc, body, carry[...], unroll=8)
    carry[...] = h
    hlast_ref[...] = h
    for bi in range(nb):
        hs = h_s[bi * pitch:bi * pitch + tc, :]
        if finalize:
            out_ref[bi] = (jax.nn.gelu(gate_ref[bi].astype(F32)) * (hf_ref[bi] + hs)).astype(out_ref.dtype)
        else:
            out_ref[bi] = hs


def _scan(xr, wa, wx, ba, bx, lam, h0, hf, rest, *, direction, gate_col):
    b, s, c = xr.shape
    nblk, bw = wa.shape[1], wa.shape[2]
    assert bw == LANES and nblk * bw == c
    tc = _tile(s, 512)
    nchunk = s // tc
    reverse = direction == 1
    finalize = hf is not None
    chunk = (lambda i: nchunk - 1 - i) if reverse else (lambda i: i)
    d = direction
    seq_spec = pl.BlockSpec((b, tc, bw), lambda n, i: (0, chunk(i), n))
    w_spec = pl.BlockSpec((1, 1, bw, bw), lambda n, i: (d, n, 0, 0))
    v_spec = pl.BlockSpec((1, 1, 1, bw), lambda n, i: (d, n, 0, 0))
    st_spec = pl.BlockSpec((b, bw), lambda n, i: (0, n))
    in_specs = [seq_spec, w_spec, w_spec, v_spec, v_spec, v_spec, st_spec]
    args = [xr, wa, wx, ba, bx, lam, h0]
    if finalize:
        in_specs += [seq_spec, pl.BlockSpec((b, tc, bw), lambda n, i: (0, chunk(i), gate_col * nblk + n))]
        args += [hf, rest]
    return pl.pallas_call(
        functools.partial(_scan_kernel, reverse=reverse, finalize=finalize),
        grid=(nblk, nchunk),
        in_specs=in_specs,
        out_specs=[seq_spec, st_spec],
        out_shape=[jax.ShapeDtypeStruct((b, s, c), BF16 if finalize else F32),
                   jax.ShapeDtypeStruct((b, c), F32)],
        scratch_shapes=[pltpu.VMEM((b * (tc + SCAN_ROW_PAD), bw), F32)] * 3 + [pltpu.VMEM((b, bw), F32)],
        compiler_params=_params("arbitrary", "arbitrary"),
        name="rglru_bwd" if reverse else "rglru_fwd",
    )(*args)


def _merge_kernel(ya_ref, yb_ref, yc_ref, g0_ref, g1_ref, g2_ref, bm_ref, wa_ref, wb_ref, wc_ref, wo_ref,
                  h_ref, mod_ref, gp_ref, o_ref):
    bm = bm_ref[...]
    wa_ref, wb_ref, wc_ref, wo_ref = wa_ref.at[0], wb_ref.at[0], wc_ref.at[0], wo_ref.at[0]
    d = wo_ref.shape[0]
    nh = 2 if d % (2 * LANES) == 0 else 1
    dh = d // nh
    merged = []
    for c in range(nh):
        cs = slice(c * dh, (c + 1) * dh)
        gate = lambda g_ref, k: jax.nn.sigmoid(g_ref[0, :, cs].astype(F32) + bm[k:k + 1, cs])
        m = gate(g0_ref, 0) * jnp.dot(ya_ref[0], wa_ref[:, cs], preferred_element_type=F32)
        m += gate(g1_ref, 1) * jnp.dot(yb_ref[0], wb_ref[:, cs], preferred_element_type=F32)
        m += gate(g2_ref, 2) * jnp.dot(yc_ref[0], wc_ref[:, cs], preferred_element_type=F32)
        merged.append(m.astype(BF16))
    out = jnp.dot(merged[0], wo_ref[0:dh, :], preferred_element_type=F32)
    for c in range(1, nh):
        out += jnp.dot(merged[c], wo_ref[c * dh:(c + 1) * dh, :], preferred_element_type=F32)
    o_ref[0] = h_ref[0] + mod_ref[0][2:3] * _rms(out, gp_ref[...])


def _merge(ya, yb, yc, gates, b_merge, wba, wbb, wbc, wo, layer, h, mods, g_post):
    b, s, d = h.shape
    tm = _tile(s, 256)
    per_batch = mods.shape[0] > 1
    row = lambda w: pl.BlockSpec((1, tm, w), lambda bi, i: (bi, i, 0))
    gate = lambda k: pl.BlockSpec((1, tm, d), lambda bi, i: (bi, i, k))
    full = lambda a: pl.BlockSpec(a.shape, lambda bi, i: (0, 0))
    weight = lambda a: pl.BlockSpec((1,) + a.shape[1:], lambda bi, i: (layer, 0, 0), pipeline_mode=pl.Buffered(1))
    return pl.pallas_call(
        _merge_kernel,
        grid=(b, s // tm),
        in_specs=[row(ya.shape[2]), row(yb.shape[2]), row(yc.shape[2]), gate(0), gate(1), gate(2),
                  full(b_merge), weight(wba), weight(wbb), weight(wbc), weight(wo), row(d),
                  pl.BlockSpec((1, N_MOD, d), (lambda bi, i: (bi, 0, 0)) if per_batch else (lambda bi, i: (0, 0, 0))),
                  pl.BlockSpec((1, d), lambda bi, i: (0, 0))],
        out_specs=row(d),
        out_shape=jax.ShapeDtypeStruct((b, s, d), F32),
        compiler_params=_params("arbitrary", "arbitrary"),
        name="merge",
    )(ya, yb, yc, gates, gates, gates, b_merge, wba, wbb, wbc, wo, h, mods, g_post)


def _ffn_kernel(h_ref, mod_ref, gpre_ref, gpost_ref, wg_ref, wu_ref, wd_ref, o_ref, un_ref, acc_ref):
    j = pl.program_id(2)

    @pl.when(j == 0)
    def _():
        m = mod_ref[0]
        un_ref[...] = (_rms(h_ref[0], gpre_ref[...]) * (1.0 + m[4:5]) + m[3:4]).astype(BF16)
        acc_ref[...] = jnp.zeros(acc_ref.shape, F32)

    u = un_ref[...]
    hg = jnp.dot(u, wg_ref[0], preferred_element_type=F32)
    hu = jnp.dot(u, wu_ref[0], preferred_element_type=F32)
    acc_ref[...] += jnp.dot((_silu(hg) * hu).astype(BF16), wd_ref[0], preferred_element_type=F32)

    @pl.when(j == pl.num_programs(2) - 1)
    def _():
        o_ref[0] = h_ref[0] + mod_ref[0][5:6] * _rms(acc_ref[...], gpost_ref[...])


def _ffn(h, mods, g_pre, g_post, wg, wu, wd, layer):
    b, s, d = h.shape
    hidden = wg.shape[2]
    tm = _tile(s, 512)
    th = _tile(hidden, 512)
    per_batch = mods.shape[0] > 1
    return pl.pallas_call(
        _ffn_kernel,
        grid=(b, s // tm, hidden // th),
        in_specs=[pl.BlockSpec((1, tm, d), lambda bi, i, j: (bi, i, 0)),
                  pl.BlockSpec((1, N_MOD, d),
                               (lambda bi, i, j: (bi, 0, 0)) if per_batch else (lambda bi, i, j: (0, 0, 0))),
                  pl.BlockSpec((1, d), lambda bi, i, j: (0, 0)),
                  pl.BlockSpec((1, d), lambda bi, i, j: (0, 0)),
                  pl.BlockSpec((1, d, th), lambda bi, i, j: (layer, 0, j)),
                  pl.BlockSpec((1, d, th), lambda bi, i, j: (layer, 0, j)),
                  pl.BlockSpec((1, th, d), lambda bi, i, j: (layer, j, 0))],
        out_specs=pl.BlockSpec((1, tm, d), lambda bi, i, j: (bi, i, 0)),
        out_shape=jax.ShapeDtypeStruct((b, s, d), F32),
        scratch_shapes=[pltpu.VMEM((tm, d), BF16), pltpu.VMEM((tm, d), F32)],
        compiler_params=_params("arbitrary", "arbitrary", "arbitrary"),
        name="ffn",
    )(h, mods, g_pre, g_post, wg, wu, wd)


def _rope_tables(n_tokens, hd):
    freqs = hd // 4
    pos = jnp.arange(n_tokens)
    rowcol = jnp.stack([(pos // GRID_W).astype(F32), (pos % GRID_W).astype(F32)], axis=1)
    inv = ROPE_BASE ** (-jnp.arange(freqs, dtype=F32) / freqs)
    lane = jnp.arange(LANES) % hd
    axis, half, f = lane // (2 * freqs), (lane % (2 * freqs)) // freqs, lane % freqs
    ang = rowcol[:, axis] * inv[f][None, :]
    cos, sin = jnp.cos(ang), jnp.sin(ang)
    return cos, jnp.where(half == 1, sin, 0.0), jnp.where(half == 0, -sin, 0.0)


def kernel(x, c, ctx, c_ctx, w_ada, b_ada, g_pre_mix, g_post_mix, g_pre_ffn, g_post_ffn, w_in, diff_lambda, diff_subln, conv_w, conv_b, rnn_conv_w, rnn_conv_b, rg_wa, rg_ba, rg_wx, rg_bx, rg_lambda, b_merge, w_branch_a, w_branch_b, w_branch_c, w_o, w_ffn_gate, w_ffn_up, w_ffn_down):
    b, s, d = x.shape
    depth = w_ada.shape[0]
    hd = diff_lambda.shape[-1]
    diff_w = w_branch_a.shape[1]
    conv_width = conv_w.shape[-1]
    rnn_width = rnn_conv_w.shape[-1]
    nblk, bw = rg_wa.shape[2], rg_wa.shape[3]
    assert diff_w == conv_width == rnn_width and 2 * hd == LANES
    gate_col_rnn = 3

    rows = -(-(b + 1) // SUBLANES) * SUBLANES
    cc = jnp.zeros((rows, d), F32).at[:b].set(c).at[b].set(c_ctx)
    mods = _ada(cc, w_ada, b_ada.reshape(depth, 1, N_MOD * d))
    tables = _rope_tables(s, hd)
    q_scale = hd ** -0.5 * math.log2(math.e)

    vec = lambda a: a.reshape(2, nblk, 1, bw)
    w_in_b = w_in.astype(BF16)
    wba, wbb, wbc, wo = (w.astype(BF16) for w in (w_branch_a, w_branch_b, w_branch_c, w_o))
    wg, wu, wd = (w.astype(BF16) for w in (w_ffn_gate, w_ffn_up, w_ffn_down))
    n_ctx = ctx.shape[1]
    flat = lambda a: a.reshape(1, b * n_ctx, a.shape[-1])
    per_batch = lambda a: a.reshape(b, n_ctx, a.shape[-1])
    h_lat, h_ctx = x, flat(ctx)
    for l in range(depth):
        need_ctx = l < depth - 1
        lam_init = 0.8 - 0.6 * math.exp(-0.3 * l)
        li = jnp.full((1, 1), lam_init, F32)
        ml = mods[l, :b].reshape(b, N_MOD, d)
        mc = mods[l, b:b + 1].reshape(1, N_MOD, d)
        g_pre = g_pre_mix[l].reshape(1, d)
        subln = diff_subln[l].reshape(2 * hd, 1)

        qk_l, vt_l, rest_l, gates_l = _inproj(h_lat, ml, g_pre, w_in_b, l, tables, width=diff_w, q_scale=q_scale)
        qk_c, vt_c, rest_c, gates_c = _inproj(h_ctx, mc, g_pre, w_in_b, l, None, width=diff_w, q_scale=q_scale)
        qk_c, rest_c = per_batch(qk_c), per_batch(rest_c)

        ya_l = _attn(qk_l, qk_c, vt_c, qk_l, vt_l, diff_lambda[l], subln, li, hd=hd)

        prep = functools.partial(_prep, conv_w=conv_w[l], conv_b=conv_b[l].reshape(1, -1),
                                 rnn_conv_w=rnn_conv_w[l], rnn_conv_b=rnn_conv_b[l].reshape(1, -1),
                                 conv_width=conv_width, rnn_width=rnn_width)
        yb_l, xr_l = prep(rest_l)
        yb_c, xr_c = prep(rest_c)

        scan = functools.partial(_scan, wa=rg_wa[l], wx=rg_wx[l], ba=vec(rg_ba[l]), bx=vec(rg_bx[l]),
                                 lam=vec(rg_lambda[l]), gate_col=gate_col_rnn)
        zero = jnp.zeros((b, rnn_width), F32)
        hf_c, hfin_f = scan(xr_c, h0=zero, hf=None, rest=None, direction=0)
        yc_c, hfin_b = scan(xr_c, h0=zero, hf=hf_c, rest=rest_c, direction=1)
        hf_l, _ = scan(xr_l, h0=hfin_f, hf=None, rest=None, direction=0)
        yc_l, _ = scan(xr_l, h0=hfin_b, hf=hf_l, rest=rest_l, direction=1)

        g_post = g_post_mix[l].reshape(1, d)
        gf_pre, gf_post = g_pre_ffn[l].reshape(1, d), g_post_ffn[l].reshape(1, d)

        h_lat = _merge(ya_l, yb_l, yc_l, gates_l, b_merge[l], wba, wbb, wbc, wo, l, h_lat, ml, g_post)
        h_lat = _ffn(h_lat, ml, gf_pre, gf_post, wg, wu, wd, l)
        if need_ctx:
            ya_c = _attn(qk_c, qk_c, vt_c, None, None, diff_lambda[l], subln, li, hd=hd)
            h_ctx = _merge(flat(ya_c), flat(yb_c), flat(yc_c), gates_c, b_merge[l], wba, wbb, wbc, wo, l, h_ctx, mc,
                           g_post)
            h_ctx = _ffn(h_ctx, mc, gf_pre, gf_post, wg, wu, wd, l)
    return h_lat
```

```python
import functools
import math

import jax
import jax.numpy as jnp
from jax import lax
from jax.experimental import pallas as pl
from jax.experimental.pallas import tpu as pltpu

GRID_W = 64
ROPE_BASE = 10000.0
EPS = 1e-6
RG_C = 8.0
N_MOD = 6
N_BRANCH = 3
N_MID = 5
LANES = 128
SUBLANES = 8
VMEM_LIMIT_BYTES = 56 * 1024 * 1024
INPROJ_CHUNK = 512
ATTN_KV_STAGE = 512
ONES_ROWS = 16
SCAN_BLOCKS = 2
SCAN_ROW_PAD = 8

F32 = jnp.float32
BF16 = jnp.bfloat16


def _tile(n, pref):
    if n <= pref:
        return n
    t = pref - pref % SUBLANES
    while t >= SUBLANES:
        if n % t == 0:
            return t
        t -= SUBLANES
    return n


def _params(*sem):
    return pltpu.CompilerParams(dimension_semantics=sem, vmem_limit_bytes=VMEM_LIMIT_BYTES)


def _rms(x, g):
    return x * lax.rsqrt(jnp.mean(x * x, axis=-1, keepdims=True) + EPS) * g


def _silu(x):
    return x * jax.nn.sigmoid(x)


def _ada_kernel(x_ref, w_ref, b_ref, o_ref):
    sx = _silu(x_ref[...]).astype(BF16)
    o_ref[0] = jnp.dot(sx, w_ref[0].astype(BF16), preferred_element_type=F32) + b_ref[0]


def _ada(cc, w_ada, b_flat):
    depth, d, n = w_ada.shape
    rows = cc.shape[0]
    tn = _tile(n, 1024)
    return pl.pallas_call(
        _ada_kernel,
        grid=(depth, n // tn),
        in_specs=[
            pl.BlockSpec((rows, d), lambda l, j: (0, 0)),
            pl.BlockSpec((1, d, tn), lambda l, j: (l, 0, j)),
            pl.BlockSpec((1, 1, tn), lambda l, j: (l, 0, j)),
        ],
        out_specs=pl.BlockSpec((1, rows, tn), lambda l, j: (l, 0, j)),
        out_shape=jax.ShapeDtypeStruct((depth, rows, n), F32),
        compiler_params=_params("arbitrary", "arbitrary"),
        name="ada",
    )(cc, w_ada, b_flat)


def _inproj_kernel(h_ref, mod_ref, g_ref, w_ref, *rest, rope, q_scale):
    if rope:
        cos_ref, s1_ref, s2_ref, qkv_ref, vt_ref, mid_ref, gates_ref, xn_ref = rest
    else:
        qkv_ref, vt_ref, mid_ref, gates_ref, xn_ref = rest
    j = pl.program_id(2)

    @pl.when(j == 0)
    def _():
        m = mod_ref[0]
        y = _rms(h_ref[0], g_ref[...])
        xn_ref[...] = (y * (1.0 + m[1:2]) + m[0:1]).astype(BF16)

    tn = w_ref.shape[2]
    cw = INPROJ_CHUNK if tn % INPROJ_CHUNK == 0 else tn

    def chunked(epilogue):
        for c in range(tn // cw):
            epilogue(c * cw, jnp.dot(xn_ref[...], w_ref[0, :, c * cw:(c + 1) * cw], preferred_element_type=F32))

    def qk_epilogue(c0, acc):
        a = acc * jnp.where(j == 0, q_scale, 1.0)
        if rope:
            cos, s1, s2 = cos_ref[...], s1_ref[...], s2_ref[...]
            quarter = LANES // 8
            for c in range(cw // LANES):
                blk = a[:, c * LANES:(c + 1) * LANES]
                r = blk * cos + pltpu.roll(blk, quarter, 1) * s1 + pltpu.roll(blk, LANES - quarter, 1) * s2
                qkv_ref[0, :, c0 + c * LANES:c0 + (c + 1) * LANES] = r.astype(BF16)
        else:
            qkv_ref[0, :, c0:c0 + cw] = a.astype(BF16)

    def v_epilogue(c0, acc):
        vt_ref[0, c0:c0 + cw, :] = acc.T.astype(BF16)

    def mid_epilogue(c0, acc):
        mid_ref[0, :, c0:c0 + cw] = acc.astype(BF16)

    def gates_epilogue(c0, acc):
        gates_ref[0, :, c0:c0 + cw] = acc.astype(BF16)

    pl.when(j < 2)(lambda: chunked(qk_epilogue))
    pl.when(j == 2)(lambda: chunked(v_epilogue))
    pl.when((j >= 3) & (j < 3 + N_MID))(lambda: chunked(mid_epilogue))
    pl.when(j >= 3 + N_MID)(lambda: chunked(gates_epilogue))


def _inproj(h, mods, g, w, layer, tables, *, width, q_scale):
    b, s, d = h.shape
    tn = width
    n = w.shape[2]
    assert n == (3 + N_MID) * width + N_BRANCH * d and d % tn == 0
    tm = _tile(s, 1024)
    rope = tables is not None
    per_batch = mods.shape[0] > 1
    in_specs = [
        pl.BlockSpec((1, tm, d), lambda bi, i, j: (bi, i, 0)),
        pl.BlockSpec((1, N_MOD, d), (lambda bi, i, j: (bi, 0, 0)) if per_batch else (lambda bi, i, j: (0, 0, 0))),
        pl.BlockSpec((1, d), lambda bi, i, j: (0, 0)),
        pl.BlockSpec((1, d, tn), lambda bi, i, j: (layer, 0, j)),
    ]
    args = [h, mods, g, w]
    if rope:
        in_specs += [pl.BlockSpec((tm, LANES), lambda bi, i, j: (i, 0))] * 3
        args += list(tables)
    return pl.pallas_call(
        functools.partial(_inproj_kernel, rope=rope, q_scale=q_scale),
        grid=(b, s // tm, n // tn),
        in_specs=in_specs,
        out_specs=[
            pl.BlockSpec((1, tm, tn), lambda bi, i, j: (bi, i, jnp.minimum(j, 1))),
            pl.BlockSpec((1, tn, tm), lambda bi, i, j: (bi, 0, i)),
            pl.BlockSpec((1, tm, tn), lambda bi, i, j: (bi, i, jnp.clip(j - 3, 0, N_MID - 1))),
            pl.BlockSpec((1, tm, tn), lambda bi, i, j: (bi, i, jnp.maximum(j - 3 - N_MID, 0))),
        ],
        out_shape=[
            jax.ShapeDtypeStruct((b, s, 2 * width), BF16),
            jax.ShapeDtypeStruct((b, width, s), BF16),
            jax.ShapeDtypeStruct((b, s, N_MID * width), BF16),
            jax.ShapeDtypeStruct((b, s, N_BRANCH * d), BF16),
        ],
        scratch_shapes=[pltpu.VMEM((tm, d), BF16)],
        compiler_params=_params("arbitrary", "arbitrary", "arbitrary"),
        name="inproj_rope" if rope else "inproj",
    )(*args)


def _attn_kernel(*refs, heads, hd, has_lat):
    if has_lat:
        (q_ref, kc_ref, vc_ref, kl_ref, vl_ref, dl_ref, sg_ref, li_ref, o_ref,
         qs_ref, m_ref, acc_ref, s_ref) = refs
    else:
        q_ref, kc_ref, vc_ref, dl_ref, sg_ref, li_ref, o_ref, qs_ref, m_ref, acc_ref, s_ref = refs
    j = pl.program_id(2)
    last = pl.num_programs(2) - 1
    tq = q_ref.shape[1]
    hw = 2 * hd

    @pl.when(j == 0)
    def _():
        lane = lax.broadcasted_iota(jnp.int32, (tq, hw), 1)
        for h in range(heads):
            q = q_ref[0, :, h * hw:(h + 1) * hw].astype(F32)
            qs_ref[h, 0:tq] = jnp.where(lane < hd, q, 0.0).astype(BF16)
            qs_ref[h, tq:2 * tq] = jnp.where(lane >= hd, q, 0.0).astype(BF16)
        m_ref[...] = jnp.full(m_ref.shape, -jnp.inf, F32)
        acc_ref[...] = jnp.zeros(acc_ref.shape, F32)

    def step(k_ref, vt_ref):
        tk = k_ref.shape[1]
        sub = min(tk, s_ref.shape[1])
        ones = jnp.ones((ONES_ROWS, sub), BF16)
        stages = [(h, c * sub) for h in range(heads) for c in range(tk // sub)]

        def scores(t):
            h, k0 = stages[t]
            k = k_ref[0, k0:k0 + sub, h * hw:(h + 1) * hw]
            s_ref[t % 2, 0:sub, :] = lax.dot_general(k, qs_ref[h], (((1,), (1,)), ((), ())),
                                                     preferred_element_type=F32)

        scores(0)
        for t, (h, k0) in enumerate(stages):
            if t + 1 < len(stages):
                scores(t + 1)
            vt = jnp.concatenate([vt_ref[0, h * hw:(h + 1) * hw, k0:k0 + sub], ones], axis=0)
            s = s_ref[t % 2, 0:sub, :]
            m_prev = m_ref[h]
            m_new = jnp.maximum(m_prev, jnp.max(s, axis=0, keepdims=True))
            alpha = jnp.exp2(m_prev - m_new)
            p = jnp.exp2(s - m_new)
            acc_ref[h] = alpha * acc_ref[h] + jnp.dot(vt, p.astype(BF16), preferred_element_type=F32)
            m_ref[h] = m_new

    if has_lat:
        @pl.when(j == 0)
        def _():
            step(kc_ref, vc_ref)

        @pl.when(j > 0)
        def _():
            step(kl_ref, vl_ref)
    else:
        step(kc_ref, vc_ref)

    @pl.when(j == last)
    def _():
        dl = dl_ref[...]
        lam_init = li_ref[...]
        lam = (jnp.exp(jnp.sum(dl[0:1] * dl[1:2], axis=-1, keepdims=True))
               - jnp.exp(jnp.sum(dl[2:3] * dl[3:4], axis=-1, keepdims=True)) + lam_init)
        for h in range(heads):
            acc = acc_ref[h, 0:hw]
            l = acc_ref[h, hw:hw + 1]
            o = acc[:, 0:tq] / l[:, 0:tq] - lam * (acc[:, tq:2 * tq] / l[:, tq:2 * tq])
            o = o * lax.rsqrt(jnp.mean(o * o, axis=0, keepdims=True) + EPS) * sg_ref[...] * (1.0 - lam_init)
            o_ref[0, :, h * hw:(h + 1) * hw] = o.T.astype(BF16)


def _attn(qk_q, qk_c, vt_c, qk_l, vt_l, diff_lambda, subln, lam_init, *, hd):
    b, sq, w2 = qk_q.shape
    width = w2 // 2
    heads = width // (2 * hd)
    n_ctx = qk_c.shape[1]
    has_lat = qk_l is not None
    tq = _tile(sq, 512)
    vc_map = (lambda bi, i, j: (bi, 0, 0)) if vt_c.shape[0] == b else (lambda bi, i, j: (0, 0, bi))
    in_specs = [
        pl.BlockSpec((1, tq, width), lambda bi, i, j: (bi, i, 0)),
        pl.BlockSpec((1, n_ctx, width), lambda bi, i, j: (bi, 0, 1)),
        pl.BlockSpec((1, width, n_ctx), vc_map),
    ]
    args = [qk_q, qk_c, vt_c]
    nkv = 1
    tk = n_ctx
    if has_lat:
        sk = qk_l.shape[1]
        tk = _tile(sk, 1024)
        assert tk >= n_ctx
        nkv += sk // tk
        in_specs += [
            pl.BlockSpec((1, tk, width), lambda bi, i, j: (bi, jnp.maximum(j - 1, 0), 1)),
            pl.BlockSpec((1, width, tk), lambda bi, i, j: (bi, 0, jnp.maximum(j - 1, 0))),
        ]
        args += [qk_l, vt_l]
    in_specs += [
        pl.BlockSpec(diff_lambda.shape, lambda bi, i, j: (0, 0)),
        pl.BlockSpec((2 * hd, 1), lambda bi, i, j: (0, 0)),
        pl.BlockSpec((1, 1), lambda bi, i, j: (0, 0)),
    ]
    args += [diff_lambda, subln, lam_init]
    return pl.pallas_call(
        functools.partial(_attn_kernel, heads=heads, hd=hd, has_lat=has_lat),
        grid=(b, sq // tq, nkv),
        in_specs=in_specs,
        out_specs=pl.BlockSpec((1, tq, width), lambda bi, i, j: (bi, i, 0)),
        out_shape=jax.ShapeDtypeStruct((b, sq, width), BF16),
        scratch_shapes=[
            pltpu.VMEM((heads, 2 * tq, 2 * hd), BF16),
            pltpu.VMEM((heads, 1, 2 * tq), F32),
            pltpu.VMEM((heads, 2 * hd + ONES_ROWS, 2 * tq), F32),
            pltpu.VMEM((2, min(tk, ATTN_KV_STAGE), 2 * tq), F32),
        ],
        compiler_params=_params("arbitrary", "arbitrary", "arbitrary"),
        name="attn_lat" if has_lat else "attn_ctx",
    )(*args)


def _prep_kernel(x3_ref, x4_ref, x5_ref, x7_ref, cw_ref, cb_ref, rw_ref, rb_ref, yb_ref, xr_ref):
    s = x3_ref.shape[1]
    row = lax.broadcasted_iota(jnp.int32, (s, x3_ref.shape[2]), 0)

    def shifted(x, k):
        r = pltpu.roll(x, k % s, 0)
        return jnp.where((row >= k) & (row < s + k), r, 0.0)

    z = x4_ref[0].astype(F32) * x3_ref[0].astype(F32)
    cw = cw_ref[...]
    conv = cw[0:1] * shifted(z, 1) + cw[1:2] * z + cw[2:3] * shifted(z, -1) + cb_ref[...]
    yb_ref[0] = (x5_ref[0].astype(F32) * conv).astype(BF16)
    x = x7_ref[0].astype(F32)
    rw = rw_ref[...]
    xr_ref[0] = (rw[0:1] * shifted(x, 2) + rw[1:2] * shifted(x, 1) + rw[2:3] * x + rw[3:4] * shifted(x, -1)
                 + rb_ref[...])


def _prep(rest, conv_w, conv_b, rnn_conv_w, rnn_conv_b, *, conv_width, rnn_width):
    b, s, _ = rest.shape
    assert conv_width == rnn_width
    tc = LANES
    nct = conv_width // tc
    col = lambda k: (lambda bi, c: (bi, 0, k * nct + c))
    par = lambda rows: pl.BlockSpec((rows, tc), lambda bi, c: (0, c))
    return pl.pallas_call(
        _prep_kernel,
        grid=(b, nct),
        in_specs=[pl.BlockSpec((1, s, tc), col(0)), pl.BlockSpec((1, s, tc), col(1)),
                  pl.BlockSpec((1, s, tc), col(2)), pl.BlockSpec((1, s, tc), col(4)),
                  par(conv_w.shape[0]), par(1), par(rnn_conv_w.shape[0]), par(1)],
        out_specs=[pl.BlockSpec((1, s, tc), lambda bi, c: (bi, 0, c)),
                   pl.BlockSpec((1, s, tc), lambda bi, c: (bi, 0, c))],
        out_shape=[jax.ShapeDtypeStruct((b, s, conv_width), BF16),
                   jax.ShapeDtypeStruct((b, s, rnn_width), F32)],
        compiler_params=_params("arbitrary", "arbitrary"),
        name="conv_prep",
    )(rest, rest, rest, rest, conv_w, conv_b, rnn_conv_w, rnn_conv_b)


def _scan_kernel(*refs, reverse, finalize):
    if finalize:
        (xr_ref, wa_ref, wx_ref, ba_ref, bx_ref, lam_ref, h0_ref, hf_ref, gate_ref, out_ref, hlast_ref,
         a_s, g_s, h_s, carry) = refs
    else:
        xr_ref, wa_ref, wx_ref, ba_ref, bx_ref, lam_ref, h0_ref, out_ref, hlast_ref, a_s, g_s, h_s, carry = refs
    i = pl.program_id(1)
    nb, tc, cw = xr_ref.shape
    ng = cw // LANES

    @pl.when(i == 0)
    def _():
        carry[...] = h0_ref[...]

    pitch = tc + SCAN_ROW_PAD
    for k in range(ng):
        lanes = slice(k * LANES, (k + 1) * LANES)
        x = xr_ref[:, :, lanes].reshape(nb * tc, LANES)
        xb = x.astype(BF16)
        r = jax.nn.sigmoid(jnp.dot(xb, wa_ref[0, k].astype(BF16), preferred_element_type=F32) + ba_ref[0, k])
        gi = jax.nn.sigmoid(jnp.dot(xb, wx_ref[0, k].astype(BF16), preferred_element_type=F32) + bx_ref[0, k])
        z = -lam_ref[0, k]
        softplus = jnp.maximum(z, 0.0) + jnp.log1p(jnp.exp(-jnp.abs(z)))
        a = jnp.exp(-RG_C * r * softplus)
        g = jnp.sqrt(1.0 - a * a) * (gi * x)
        for bi in range(nb):
            a_s[k, bi * pitch:bi * pitch + tc, :] = a[bi * tc:(bi + 1) * tc]
            g_s[k, bi * pitch:bi * pitch + tc, :] = g[bi * tc:(bi + 1) * tc]

    def body(t, hs):
        tt = tc - 1 - t if reverse else t
        rows = pl.ds(tt, nb, stride=pitch)
        new = []
        for k in range(ng):
            h = a_s[k, rows, :] * hs[k] + g_s[k, rows, :]
            h_s[k, rows, :] = h
            new.append(h)
        return tuple(new)

    hs = lax.fori_loop(0, tc, body, tuple(carry[:, k * LANES:(k + 1) * LANES] for k in range(ng)), unroll=8)
    for k in range(ng):
        lanes = slice(k * LANES, (k + 1) * LANES)
        carry[:, lanes] = hs[k]
        hlast_ref[:, lanes] = hs[k]
        for bi in range(nb):
            hb = h_s[k, bi * pitch:bi * pitch + tc, :]
            if finalize:
                out_ref[bi, :, lanes] = (jax.nn.gelu(gate_ref[bi, :, lanes].astype(F32))
                                         * (hf_ref[bi, :, lanes] + hb)).astype(out_ref.dtype)
            else:
                out_ref[bi, :, lanes] = hb


def _scan(xr, wa, wx, ba, bx, lam, h0, hf, rest, *, direction, gate_col):
    b, s, c = xr.shape
    nblk, bw = wa.shape[1], wa.shape[2]
    assert bw == LANES and nblk * bw == c
    tc = _tile(s, 512)
    nchunk = s // tc
    reverse = direction == 1
    finalize = hf is not None
    chunk = (lambda i: nchunk - 1 - i) if reverse else (lambda i: i)
    d = direction
    ng = SCAN_BLOCKS if nblk % SCAN_BLOCKS == 0 else 1
    gw = ng * bw
    seq_spec = pl.BlockSpec((b, tc, gw), lambda n, i: (0, chunk(i), n))
    w_spec = pl.BlockSpec((1, ng, bw, bw), lambda n, i: (d, n, 0, 0))
    v_spec = pl.BlockSpec((1, ng, 1, bw), lambda n, i: (d, n, 0, 0))
    st_spec = pl.BlockSpec((b, gw), lambda n, i: (0, n))
    in_specs = [seq_spec, w_spec, w_spec, v_spec, v_spec, v_spec, st_spec]
    args = [xr, wa, wx, ba, bx, lam, h0]
    if finalize:
        in_specs += [seq_spec, pl.BlockSpec((b, tc, gw), lambda n, i: (0, chunk(i), gate_col * (nblk // ng) + n))]
        args += [hf, rest]
    return pl.pallas_call(
        functools.partial(_scan_kernel, reverse=reverse, finalize=finalize),
        grid=(nblk // ng, nchunk),
        in_specs=in_specs,
        out_specs=[seq_spec, st_spec],
        out_shape=[jax.ShapeDtypeStruct((b, s, c), BF16 if finalize else F32),
                   jax.ShapeDtypeStruct((b, c), F32)],
        scratch_shapes=[pltpu.VMEM((ng, b * (tc + SCAN_ROW_PAD), bw), F32)] * 3 + [pltpu.VMEM((b, gw), F32)],
        compiler_params=_params("arbitrary", "arbitrary"),
        name="rglru_bwd" if reverse else "rglru_fwd",
    )(*args)


def _merge_kernel(ya_ref, yb_ref, yc_ref, g0_ref, g1_ref, g2_ref, bm_ref, wa_ref, wb_ref, wc_ref, wo_ref,
                  h_ref, mod_ref, gp_ref, o_ref):
    bm = bm_ref[...]
    wa_ref, wb_ref, wc_ref, wo_ref = wa_ref.at[0], wb_ref.at[0], wc_ref.at[0], wo_ref.at[0]
    d = wo_ref.shape[0]
    nh = 2 if d % (2 * LANES) == 0 else 1
    dh = d // nh
    merged = []
    for c in range(nh):
        cs = slice(c * dh, (c + 1) * dh)
        gate = lambda g_ref, k: jax.nn.sigmoid(g_ref[0, :, cs].astype(F32) + bm[k:k + 1, cs])
        m = gate(g0_ref, 0) * jnp.dot(ya_ref[0], wa_ref[:, cs], preferred_element_type=F32)
        m += gate(g1_ref, 1) * jnp.dot(yb_ref[0], wb_ref[:, cs], preferred_element_type=F32)
        m += gate(g2_ref, 2) * jnp.dot(yc_ref[0], wc_ref[:, cs], preferred_element_type=F32)
        merged.append(m.astype(BF16))
    out = jnp.dot(merged[0], wo_ref[0:dh, :], preferred_element_type=F32)
    for c in range(1, nh):
        out += jnp.dot(merged[c], wo_ref[c * dh:(c + 1) * dh, :], preferred_element_type=F32)
    o_ref[0] = h_ref[0] + mod_ref[0][2:3] * _rms(out, gp_ref[...])


def _merge(ya, yb, yc, gates, b_merge, wba, wbb, wbc, wo, layer, h, mods, g_post):
    b, s, d = h.shape
    tm = _tile(s, 256)
    per_batch = mods.shape[0] > 1
    row = lambda w: pl.BlockSpec((1, tm, w), lambda bi, i: (bi, i, 0))
    gate = lambda k: pl.BlockSpec((1, tm, d), lambda bi, i: (bi, i, k))
    full = lambda a: pl.BlockSpec(a.shape, lambda bi, i: (0, 0))
    weight = lambda a: pl.BlockSpec((1,) + a.shape[1:], lambda bi, i: (layer, 0, 0), pipeline_mode=pl.Buffered(1))
    return pl.pallas_call(
        _merge_kernel,
        grid=(b, s // tm),
        in_specs=[row(ya.shape[2]), row(yb.shape[2]), row(yc.shape[2]), gate(0), gate(1), gate(2),
                  full(b_merge), weight(wba), weight(wbb), weight(wbc), weight(wo), row(d),
                  pl.BlockSpec((1, N_MOD, d), (lambda bi, i: (bi, 0, 0)) if per_batch else (lambda bi, i: (0, 0, 0))),
                  pl.BlockSpec((1, d), lambda bi, i: (0, 0))],
        out_specs=row(d),
        out_shape=jax.ShapeDtypeStruct((b, s, d), F32),
        compiler_params=_params("arbitrary", "arbitrary"),
        name="merge",
    )(ya, yb, yc, gates, gates, gates, b_merge, wba, wbb, wbc, wo, h, mods, g_post)


def _ffn_kernel(h_ref, mod_ref, gpre_ref, gpost_ref, wg_ref, wu_ref, wd_ref, o_ref, un_ref, acc_ref):
    j = pl.program_id(2)

    @pl.when(j == 0)
    def _():
        m = mod_ref[0]
        un_ref[...] = (_rms(h_ref[0], gpre_ref[...]) * (1.0 + m[4:5]) + m[3:4]).astype(BF16)
        acc_ref[...] = jnp.zeros(acc_ref.shape, F32)

    u = un_ref[...]
    hg = jnp.dot(u, wg_ref[0], preferred_element_type=F32)
    hu = jnp.dot(u, wu_ref[0], preferred_element_type=F32)
    acc_ref[...] += jnp.dot((_silu(hg) * hu).astype(BF16), wd_ref[0], preferred_element_type=F32)

    @pl.when(j == pl.num_programs(2) - 1)
    def _():
        o_ref[0] = h_ref[0] + mod_ref[0][5:6] * _rms(acc_ref[...], gpost_ref[...])


def _ffn(h, mods, g_pre, g_post, wg, wu, wd, layer):
    b, s, d = h.shape
    hidden = wg.shape[2]
    tm = _tile(s, 512)
    th = _tile(hidden, 512)
    per_batch = mods.shape[0] > 1
    return pl.pallas_call(
        _ffn_kernel,
        grid=(b, s // tm, hidden // th),
        in_specs=[pl.BlockSpec((1, tm, d), lambda bi, i, j: (bi, i, 0)),
                  pl.BlockSpec((1, N_MOD, d),
                               (lambda bi, i, j: (bi, 0, 0)) if per_batch else (lambda bi, i, j: (0, 0, 0))),
                  pl.BlockSpec((1, d), lambda bi, i, j: (0, 0)),
                  pl.BlockSpec((1, d), lambda bi, i, j: (0, 0)),
                  pl.BlockSpec((1, d, th), lambda bi, i, j: (layer, 0, j)),
                  pl.BlockSpec((1, d, th), lambda bi, i, j: (layer, 0, j)),
                  pl.BlockSpec((1, th, d), lambda bi, i, j: (layer, j, 0))],
        out_specs=pl.BlockSpec((1, tm, d), lambda bi, i, j: (bi, i, 0)),
        out_shape=jax.ShapeDtypeStruct((b, s, d), F32),
        scratch_shapes=[pltpu.VMEM((tm, d), BF16), pltpu.VMEM((tm, d), F32)],
        compiler_params=_params("arbitrary", "arbitrary", "arbitrary"),
        name="ffn",
    )(h, mods, g_pre, g_post, wg, wu, wd)


def _rope_tables(n_tokens, hd):
    freqs = hd // 4
    pos = jnp.arange(n_tokens)
    rowcol = jnp.stack([(pos // GRID_W).astype(F32), (pos % GRID_W).astype(F32)], axis=1)
    inv = ROPE_BASE ** (-jnp.arange(freqs, dtype=F32) / freqs)
    lane = jnp.arange(LANES) % hd
    axis, half, f = lane // (2 * freqs), (lane % (2 * freqs)) // freqs, lane % freqs
    ang = rowcol[:, axis] * inv[f][None, :]
    cos, sin = jnp.cos(ang), jnp.sin(ang)
    return cos, jnp.where(half == 1, sin, 0.0), jnp.where(half == 0, -sin, 0.0)


def kernel(x, c, ctx, c_ctx, w_ada, b_ada, g_pre_mix, g_post_mix, g_pre_ffn, g_post_ffn, w_in, diff_lambda, diff_subln, conv_w, conv_b, rnn_conv_w, rnn_conv_b, rg_wa, rg_ba, rg_wx, rg_bx, rg_lambda, b_merge, w_branch_a, w_branch_b, w_branch_c, w_o, w_ffn_gate, w_ffn_up, w_ffn_down):
    b, s, d = x.shape
    depth = w_ada.shape[0]
    hd = diff_lambda.shape[-1]
    diff_w = w_branch_a.shape[1]
    conv_width = conv_w.shape[-1]
    rnn_width = rnn_conv_w.shape[-1]
    nblk, bw = rg_wa.shape[2], rg_wa.shape[3]
    assert diff_w == conv_width == rnn_width and 2 * hd == LANES
    gate_col_rnn = 3

    rows = -(-(b + 1) // SUBLANES) * SUBLANES
    cc = jnp.zeros((rows, d), F32).at[:b].set(c).at[b].set(c_ctx)
    mods = _ada(cc, w_ada, b_ada.reshape(depth, 1, N_MOD * d))
    tables = _rope_tables(s, hd)
    q_scale = hd ** -0.5 * math.log2(math.e)

    vec = lambda a: a.reshape(2, nblk, 1, bw)
    w_in_b = w_in.astype(BF16)
    wba, wbb, wbc, wo = (w.astype(BF16) for w in (w_branch_a, w_branch_b, w_branch_c, w_o))
    wg, wu, wd = (w.astype(BF16) for w in (w_ffn_gate, w_ffn_up, w_ffn_down))
    n_ctx = ctx.shape[1]
    flat = lambda a: a.reshape(1, b * n_ctx, a.shape[-1])
    per_batch = lambda a: a.reshape(b, n_ctx, a.shape[-1])
    h_lat, h_ctx = x, flat(ctx)
    for l in range(depth):
        need_ctx = l < depth - 1
        lam_init = 0.8 - 0.6 * math.exp(-0.3 * l)
        li = jnp.full((1, 1), lam_init, F32)
        ml = mods[l, :b].reshape(b, N_MOD, d)
        mc = mods[l, b:b + 1].reshape(1, N_MOD, d)
        g_pre = g_pre_mix[l].reshape(1, d)
        subln = diff_subln[l].reshape(2 * hd, 1)

        qk_l, vt_l, rest_l, gates_l = _inproj(h_lat, ml, g_pre, w_in_b, l, tables, width=diff_w, q_scale=q_scale)
        qk_c, vt_c, rest_c, gates_c = _inproj(h_ctx, mc, g_pre, w_in_b, l, None, width=diff_w, q_scale=q_scale)
        qk_c, rest_c = per_batch(qk_c), per_batch(rest_c)

        ya_l = _attn(qk_l, qk_c, vt_c, qk_l, vt_l, diff_lambda[l], subln, li, hd=hd)

        prep = functools.partial(_prep, conv_w=conv_w[l], conv_b=conv_b[l].reshape(1, -1),
                                 rnn_conv_w=rnn_conv_w[l], rnn_conv_b=rnn_conv_b[l].reshape(1, -1),
                                 conv_width=conv_width, rnn_width=rnn_width)
        yb_l, xr_l = prep(rest_l)
        yb_c, xr_c = prep(rest_c)

        scan = functools.partial(_scan, wa=rg_wa[l], wx=rg_wx[l], ba=vec(rg_ba[l]), bx=vec(rg_bx[l]),
                                 lam=vec(rg_lambda[l]), gate_col=gate_col_rnn)
        zero = jnp.zeros((b, rnn_width), F32)
        hf_c, hfin_f = scan(xr_c, h0=zero, hf=None, rest=None, direction=0)
        yc_c, hfin_b = scan(xr_c, h0=zero, hf=hf_c, rest=rest_c, direction=1)
        hf_l, _ = scan(xr_l, h0=hfin_f, hf=None, rest=None, direction=0)
        yc_l, _ = scan(xr_l, h0=hfin_b, hf=hf_l, rest=rest_l, direction=1)

        g_post = g_post_mix[l].reshape(1, d)
        gf_pre, gf_post = g_pre_ffn[l].reshape(1, d), g_post_ffn[l].reshape(1, d)

        h_lat = _merge(ya_l, yb_l, yc_l, gates_l, b_merge[l], wba, wbb, wbc, wo, l, h_lat, ml, g_post)
        h_lat = _ffn(h_lat, ml, gf_pre, gf_post, wg, wu, wd, l)
        if need_ctx:
            ya_c = _attn(qk_c, qk_c, vt_c, None, None, diff_lambda[l], subln, li, hd=hd)
            h_ctx = _merge(flat(ya_c), flat(yb_c), flat(yc_c), gates_c, b_merge[l], wba, wbb, wbc, wo, l, h_ctx, mc,
                           g_post)
            h_ctx = _ffn(h_ctx, mc, gf_pre, gf_post, wg, wu, wd, l)
    return h_lat
```

```python
import functools
import math

import jax
import jax.numpy as jnp
from jax import lax
from jax.experimental import pallas as pl
from jax.experimental.pallas import tpu as pltpu

GRID_W = 64
ROPE_BASE = 10000.0
EPS = 1e-6
RG_C = 8.0
N_MOD = 6
N_BRANCH = 3
N_MID = 5
LANES = 128
SUBLANES = 8
VMEM_LIMIT_BYTES = 56 * 1024 * 1024
INPROJ_CHUNK = 512
ATTN_KV_STAGE = 512
ONES_ROWS = 16
SCAN_BLOCKS = 2
SCAN_ROW_PAD = 8

F32 = jnp.float32
BF16 = jnp.bfloat16


def _tile(n, pref):
    if n <= pref:
        return n
    t = pref - pref % SUBLANES
    while t >= SUBLANES:
        if n % t == 0:
            return t
        t -= SUBLANES
    return n


def _params(*sem):
    return pltpu.CompilerParams(dimension_semantics=sem, vmem_limit_bytes=VMEM_LIMIT_BYTES)


def _rms(x, g):
    return x * lax.rsqrt(jnp.mean(x * x, axis=-1, keepdims=True) + EPS) * g


def _silu(x):
    return x * jax.nn.sigmoid(x)


def _ada_kernel(x_ref, w_ref, b_ref, o_ref):
    sx = _silu(x_ref[...]).astype(BF16)
    o_ref[0] = jnp.dot(sx, w_ref[0].astype(BF16), preferred_element_type=F32) + b_ref[0]


def _ada(cc, w_ada, b_flat):
    depth, d, n = w_ada.shape
    rows = cc.shape[0]
    tn = _tile(n, 1024)
    return pl.pallas_call(
        _ada_kernel,
        grid=(depth, n // tn),
        in_specs=[
            pl.BlockSpec((rows, d), lambda l, j: (0, 0)),
            pl.BlockSpec((1, d, tn), lambda l, j: (l, 0, j)),
            pl.BlockSpec((1, 1, tn), lambda l, j: (l, 0, j)),
        ],
        out_specs=pl.BlockSpec((1, rows, tn), lambda l, j: (l, 0, j)),
        out_shape=jax.ShapeDtypeStruct((depth, rows, n), F32),
        compiler_params=_params("arbitrary", "arbitrary"),
        name="ada",
    )(cc, w_ada, b_flat)


def _inproj_kernel(h_ref, mod_ref, g_ref, w_ref, *rest, rope, q_scale):
    if rope:
        cos_ref, s1_ref, s2_ref, qkv_ref, vt_ref, mid_ref, gates_ref, xn_ref = rest
    else:
        qkv_ref, vt_ref, mid_ref, gates_ref, xn_ref = rest
    j = pl.program_id(2)

    @pl.when(j == 0)
    def _():
        m = mod_ref[0]
        y = _rms(h_ref[0], g_ref[...])
        xn_ref[...] = (y * (1.0 + m[1:2]) + m[0:1]).astype(BF16)

    tn = w_ref.shape[2]
    cw = INPROJ_CHUNK if tn % INPROJ_CHUNK == 0 else tn

    def chunked(epilogue):
        for c in range(tn // cw):
            epilogue(c * cw, jnp.dot(xn_ref[...], w_ref[0, :, c * cw:(c + 1) * cw], preferred_element_type=F32))

    def qk_epilogue(c0, acc):
        a = acc * jnp.where(j == 0, q_scale, 1.0)
        if rope:
            cos, s1, s2 = cos_ref[...], s1_ref[...], s2_ref[...]
            quarter = LANES // 8
            for c in range(cw // LANES):
                blk = a[:, c * LANES:(c + 1) * LANES]
                r = blk * cos + pltpu.roll(blk, quarter, 1) * s1 + pltpu.roll(blk, LANES - quarter, 1) * s2
                qkv_ref[0, :, c0 + c * LANES:c0 + (c + 1) * LANES] = r.astype(BF16)
        else:
            qkv_ref[0, :, c0:c0 + cw] = a.astype(BF16)

    def v_epilogue(c0, acc):
        vt_ref[0, c0:c0 + cw, :] = acc.T.astype(BF16)

    def mid_epilogue(c0, acc):
        mid_ref[0, :, c0:c0 + cw] = acc.astype(BF16)

    def gates_epilogue(c0, acc):
        gates_ref[0, :, c0:c0 + cw] = acc.astype(BF16)

    pl.when(j < 2)(lambda: chunked(qk_epilogue))
    pl.when(j == 2)(lambda: chunked(v_epilogue))
    pl.when((j >= 3) & (j < 3 + N_MID))(lambda: chunked(mid_epilogue))
    pl.when(j >= 3 + N_MID)(lambda: chunked(gates_epilogue))


def _inproj(h, mods, g, w, layer, tables, *, width, q_scale):
    b, s, d = h.shape
    tn = width
    n = w.shape[2]
    assert n == (3 + N_MID) * width + N_BRANCH * d and d % tn == 0
    tm = _tile(s, 1024)
    rope = tables is not None
    per_batch = mods.shape[0] > 1
    in_specs = [
        pl.BlockSpec((1, tm, d), lambda bi, i, j: (bi, i, 0)),
        pl.BlockSpec((1, N_MOD, d), (lambda bi, i, j: (bi, 0, 0)) if per_batch else (lambda bi, i, j: (0, 0, 0))),
        pl.BlockSpec((1, d), lambda bi, i, j: (0, 0)),
        pl.BlockSpec((1, d, tn), lambda bi, i, j: (layer, 0, j)),
    ]
    args = [h, mods, g, w]
    if rope:
        in_specs += [pl.BlockSpec((tm, LANES), lambda bi, i, j: (i, 0))] * 3
        args += list(tables)
    return pl.pallas_call(
        functools.partial(_inproj_kernel, rope=rope, q_scale=q_scale),
        grid=(b, s // tm, n // tn),
        in_specs=in_specs,
        out_specs=[
            pl.BlockSpec((1, tm, tn), lambda bi, i, j: (bi, i, jnp.minimum(j, 1))),
            pl.BlockSpec((1, tn, tm), lambda bi, i, j: (bi, 0, i)),
            pl.BlockSpec((1, tm, tn), lambda bi, i, j: (bi, i, jnp.clip(j - 3, 0, N_MID - 1))),
            pl.BlockSpec((1, tm, tn), lambda bi, i, j: (bi, i, jnp.maximum(j - 3 - N_MID, 0))),
        ],
        out_shape=[
            jax.ShapeDtypeStruct((b, s, 2 * width), BF16),
            jax.ShapeDtypeStruct((b, width, s), BF16),
            jax.ShapeDtypeStruct((b, s, N_MID * width), BF16),
            jax.ShapeDtypeStruct((b, s, N_BRANCH * d), BF16),
        ],
        scratch_shapes=[pltpu.VMEM((tm, d), BF16)],
        compiler_params=_params("arbitrary", "arbitrary", "arbitrary"),
        name="inproj_rope" if rope else "inproj",
    )(*args)


def _attn_kernel(*refs, heads, hd, has_lat):
    if has_lat:
        (q_ref, kc_ref, vc_ref, kl_ref, vl_ref, dl_ref, sg_ref, li_ref, o_ref,
         qs_ref, m_ref, acc_ref, s_ref) = refs
    else:
        q_ref, kc_ref, vc_ref, dl_ref, sg_ref, li_ref, o_ref, qs_ref, m_ref, acc_ref, s_ref = refs
    j = pl.program_id(2)
    last = pl.num_programs(2) - 1
    tq = q_ref.shape[1]
    hw = 2 * hd

    @pl.when(j == 0)
    def _():
        lane = lax.broadcasted_iota(jnp.int32, (tq, hw), 1)
        for h in range(heads):
            q = q_ref[0, :, h * hw:(h + 1) * hw].astype(F32)
            qs_ref[h, 0:tq] = jnp.where(lane < hd, q, 0.0).astype(BF16)
            qs_ref[h, tq:2 * tq] = jnp.where(lane >= hd, q, 0.0).astype(BF16)
        m_ref[...] = jnp.full(m_ref.shape, -jnp.inf, F32)
        acc_ref[...] = jnp.zeros(acc_ref.shape, F32)

    def step(k_ref, vt_ref):
        tk = k_ref.shape[1]
        sub = min(tk, s_ref.shape[1])
        ones = jnp.ones((ONES_ROWS, sub), BF16)
        stages = [(h, c * sub) for h in range(heads) for c in range(tk // sub)]

        def scores(t):
            h, k0 = stages[t]
            k = k_ref[0, k0:k0 + sub, h * hw:(h + 1) * hw]
            s_ref[t % 2, 0:sub, :] = lax.dot_general(k, qs_ref[h], (((1,), (1,)), ((), ())),
                                                     preferred_element_type=F32)

        scores(0)
        for t, (h, k0) in enumerate(stages):
            if t + 1 < len(stages):
                scores(t + 1)
            vt = jnp.concatenate([vt_ref[0, h * hw:(h + 1) * hw, k0:k0 + sub], ones], axis=0)
            s = s_ref[t % 2, 0:sub, :]
            m_prev = m_ref[h]
            m_new = jnp.maximum(m_prev, jnp.max(s, axis=0, keepdims=True))
            alpha = jnp.exp2(m_prev - m_new)
            p = jnp.exp2(s - m_new)
            acc_ref[h] = alpha * acc_ref[h] + jnp.dot(vt, p.astype(BF16), preferred_element_type=F32)
            m_ref[h] = m_new

    if has_lat:
        @pl.when(j == 0)
        def _():
            step(kc_ref, vc_ref)

        @pl.when(j > 0)
        def _():
            step(kl_ref, vl_ref)
    else:
        step(kc_ref, vc_ref)

    @pl.when(j == last)
    def _():
        dl = dl_ref[...]
        lam_init = li_ref[...]
        lam = (jnp.exp(jnp.sum(dl[0:1] * dl[1:2], axis=-1, keepdims=True))
               - jnp.exp(jnp.sum(dl[2:3] * dl[3:4], axis=-1, keepdims=True)) + lam_init)
        for h in range(heads):
            acc = acc_ref[h, 0:hw]
            rl = 1.0 / acc_ref[h, hw:hw + 1]
            o = acc[:, 0:tq] * rl[:, 0:tq] - (lam * rl[:, tq:2 * tq]) * acc[:, tq:2 * tq]
            o = o * lax.rsqrt(jnp.mean(o * o, axis=0, keepdims=True) + EPS) * sg_ref[...] * (1.0 - lam_init)
            o_ref[0, :, h * hw:(h + 1) * hw] = o.T.astype(BF16)


def _attn(qk_q, qk_c, vt_c, qk_l, vt_l, diff_lambda, subln, lam_init, *, hd):
    b, sq, w2 = qk_q.shape
    width = w2 // 2
    heads = width // (2 * hd)
    n_ctx = qk_c.shape[1]
    has_lat = qk_l is not None
    tq = _tile(sq, 512)
    vc_map = (lambda bi, i, j: (bi, 0, 0)) if vt_c.shape[0] == b else (lambda bi, i, j: (0, 0, bi))
    in_specs = [
        pl.BlockSpec((1, tq, width), lambda bi, i, j: (bi, i, 0)),
        pl.BlockSpec((1, n_ctx, width), lambda bi, i, j: (bi, 0, 1)),
        pl.BlockSpec((1, width, n_ctx), vc_map),
    ]
    args = [qk_q, qk_c, vt_c]
    nkv = 1
    tk = n_ctx
    if has_lat:
        sk = qk_l.shape[1]
        tk = _tile(sk, 1024)
        assert tk >= n_ctx
        nkv += sk // tk
        in_specs += [
            pl.BlockSpec((1, tk, width), lambda bi, i, j: (bi, jnp.maximum(j - 1, 0), 1)),
            pl.BlockSpec((1, width, tk), lambda bi, i, j: (bi, 0, jnp.maximum(j - 1, 0))),
        ]
        args += [qk_l, vt_l]
    in_specs += [
        pl.BlockSpec(diff_lambda.shape, lambda bi, i, j: (0, 0)),
        pl.BlockSpec((2 * hd, 1), lambda bi, i, j: (0, 0)),
        pl.BlockSpec((1, 1), lambda bi, i, j: (0, 0)),
    ]
    args += [diff_lambda, subln, lam_init]
    return pl.pallas_call(
        functools.partial(_attn_kernel, heads=heads, hd=hd, has_lat=has_lat),
        grid=(b, sq // tq, nkv),
        in_specs=in_specs,
        out_specs=pl.BlockSpec((1, tq, width), lambda bi, i, j: (bi, i, 0)),
        out_shape=jax.ShapeDtypeStruct((b, sq, width), BF16),
        scratch_shapes=[
            pltpu.VMEM((heads, 2 * tq, 2 * hd), BF16),
            pltpu.VMEM((heads, 1, 2 * tq), F32),
            pltpu.VMEM((heads, 2 * hd + ONES_ROWS, 2 * tq), F32),
            pltpu.VMEM((2, min(tk, ATTN_KV_STAGE), 2 * tq), F32),
        ],
        compiler_params=_params("arbitrary", "arbitrary", "arbitrary"),
        name="attn_lat" if has_lat else "attn_ctx",
    )(*args)


def _prep_kernel(x3_ref, x4_ref, x5_ref, x7_ref, cw_ref, cb_ref, rw_ref, rb_ref, yb_ref, xr_ref):
    s = x3_ref.shape[1]
    row = lax.broadcasted_iota(jnp.int32, (s, x3_ref.shape[2]), 0)

    def shifted(x, k):
        r = pltpu.roll(x, k % s, 0)
        return jnp.where((row >= k) & (row < s + k), r, 0.0)

    z = x4_ref[0].astype(F32) * x3_ref[0].astype(F32)
    cw = cw_ref[...]
    conv = cw[0:1] * shifted(z, 1) + cw[1:2] * z + cw[2:3] * shifted(z, -1) + cb_ref[...]
    yb_ref[0] = (x5_ref[0].astype(F32) * conv).astype(BF16)
    x = x7_ref[0].astype(F32)
    rw = rw_ref[...]
    xr_ref[0] = (rw[0:1] * shifted(x, 2) + rw[1:2] * shifted(x, 1) + rw[2:3] * x + rw[3:4] * shifted(x, -1)
                 + rb_ref[...])


def _prep(rest, conv_w, conv_b, rnn_conv_w, rnn_conv_b, *, conv_width, rnn_width):
    b, s, _ = rest.shape
    assert conv_width == rnn_width
    tc = LANES
    nct = conv_width // tc
    col = lambda k: (lambda bi, c: (bi, 0, k * nct + c))
    par = lambda rows: pl.BlockSpec((rows, tc), lambda bi, c: (0, c))
    return pl.pallas_call(
        _prep_kernel,
        grid=(b, nct),
        in_specs=[pl.BlockSpec((1, s, tc), col(0)), pl.BlockSpec((1, s, tc), col(1)),
                  pl.BlockSpec((1, s, tc), col(2)), pl.BlockSpec((1, s, tc), col(4)),
                  par(conv_w.shape[0]), par(1), par(rnn_conv_w.shape[0]), par(1)],
        out_specs=[pl.BlockSpec((1, s, tc), lambda bi, c: (bi, 0, c)),
                   pl.BlockSpec((1, s, tc), lambda bi, c: (bi, 0, c))],
        out_shape=[jax.ShapeDtypeStruct((b, s, conv_width), BF16),
                   jax.ShapeDtypeStruct((b, s, rnn_width), F32)],
        compiler_params=_params("arbitrary", "arbitrary"),
        name="conv_prep",
    )(rest, rest, rest, rest, conv_w, conv_b, rnn_conv_w, rnn_conv_b)


def _scan_kernel(*refs, reverse, finalize):
    if finalize:
        (xr_ref, wa_ref, wx_ref, ba_ref, bx_ref, lam_ref, h0_ref, hf_ref, gate_ref, out_ref, hlast_ref,
         a_s, g_s, h_s, carry) = refs
    else:
        xr_ref, wa_ref, wx_ref, ba_ref, bx_ref, lam_ref, h0_ref, out_ref, hlast_ref, a_s, g_s, h_s, carry = refs
    i = pl.program_id(1)
    nb, tc, cw = xr_ref.shape
    ng = cw // LANES

    @pl.when(i == 0)
    def _():
        carry[...] = h0_ref[...]

    pitch = tc + SCAN_ROW_PAD
    for k in range(ng):
        lanes = slice(k * LANES, (k + 1) * LANES)
        x = xr_ref[:, :, lanes].reshape(nb * tc, LANES)
        xb = x.astype(BF16)
        r = jax.nn.sigmoid(jnp.dot(xb, wa_ref[0, k].astype(BF16), preferred_element_type=F32) + ba_ref[0, k])
        gi = jax.nn.sigmoid(jnp.dot(xb, wx_ref[0, k].astype(BF16), preferred_element_type=F32) + bx_ref[0, k])
        z = -lam_ref[0, k]
        softplus = jnp.maximum(z, 0.0) + jnp.log1p(jnp.exp(-jnp.abs(z)))
        a = jnp.exp2(r * ((-RG_C * math.log2(math.e)) * softplus))
        g = jnp.sqrt(1.0 - a * a) * (gi * x)
        for bi in range(nb):
            a_s[k, bi * pitch:bi * pitch + tc, :] = a[bi * tc:(bi + 1) * tc]
            g_s[k, bi * pitch:bi * pitch + tc, :] = g[bi * tc:(bi + 1) * tc]

    def body(t, hs):
        tt = tc - 1 - t if reverse else t
        rows = pl.ds(tt, nb, stride=pitch)
        new = []
        for k in range(ng):
            h = a_s[k, rows, :] * hs[k] + g_s[k, rows, :]
            h_s[k, rows, :] = h
            new.append(h)
        return tuple(new)

    hs = lax.fori_loop(0, tc, body, tuple(carry[:, k * LANES:(k + 1) * LANES] for k in range(ng)), unroll=8)
    for k in range(ng):
        lanes = slice(k * LANES, (k + 1) * LANES)
        carry[:, lanes] = hs[k]
        hlast_ref[:, lanes] = hs[k]
        for bi in range(nb):
            hb = h_s[k, bi * pitch:bi * pitch + tc, :]
            if finalize:
                out_ref[bi, :, lanes] = (jax.nn.gelu(gate_ref[bi, :, lanes].astype(F32))
                                         * (hf_ref[bi, :, lanes] + hb)).astype(out_ref.dtype)
            else:
                out_ref[bi, :, lanes] = hb


def _scan(xr, wa, wx, ba, bx, lam, h0, hf, rest, *, direction, gate_col):
    b, s, c = xr.shape
    nblk, bw = wa.shape[1], wa.shape[2]
    assert bw == LANES and nblk * bw == c
    tc = _tile(s, 512)
    nchunk = s // tc
    reverse = direction == 1
    finalize = hf is not None
    chunk = (lambda i: nchunk - 1 - i) if reverse else (lambda i: i)
    d = direction
    ng = SCAN_BLOCKS if nblk % SCAN_BLOCKS == 0 else 1
    gw = ng * bw
    seq_spec = pl.BlockSpec((b, tc, gw), lambda n, i: (0, chunk(i), n))
    w_spec = pl.BlockSpec((1, ng, bw, bw), lambda n, i: (d, n, 0, 0))
    v_spec = pl.BlockSpec((1, ng, 1, bw), lambda n, i: (d, n, 0, 0))
    st_spec = pl.BlockSpec((b, gw), lambda n, i: (0, n))
    in_specs = [seq_spec, w_spec, w_spec, v_spec, v_spec, v_spec, st_spec]
    args = [xr, wa, wx, ba, bx, lam, h0]
    if finalize:
        in_specs += [seq_spec, pl.BlockSpec((b, tc, gw), lambda n, i: (0, chunk(i), gate_col * (nblk // ng) + n))]
        args += [hf, rest]
    return pl.pallas_call(
        functools.partial(_scan_kernel, reverse=reverse, finalize=finalize),
        grid=(nblk // ng, nchunk),
        in_specs=in_specs,
        out_specs=[seq_spec, st_spec],
        out_shape=[jax.ShapeDtypeStruct((b, s, c), BF16 if finalize else F32),
                   jax.ShapeDtypeStruct((b, c), F32)],
        scratch_shapes=[pltpu.VMEM((ng, b * (tc + SCAN_ROW_PAD), bw), F32)] * 3 + [pltpu.VMEM((b, gw), F32)],
        compiler_params=_params("arbitrary", "arbitrary"),
        name="rglru_bwd" if reverse else "rglru_fwd",
    )(*args)


def _merge_kernel(ya_ref, yb_ref, yc_ref, g0_ref, g1_ref, g2_ref, bm_ref, wa_ref, wb_ref, wc_ref, wo_ref,
                  h_ref, mod_ref, gp_ref, o_ref):
    bm = bm_ref[...]
    wa_ref, wb_ref, wc_ref, wo_ref = wa_ref.at[0], wb_ref.at[0], wc_ref.at[0], wo_ref.at[0]
    d = wo_ref.shape[0]
    nh = 2 if d % (2 * LANES) == 0 else 1
    dh = d // nh
    merged = []
    for c in range(nh):
        cs = slice(c * dh, (c + 1) * dh)
        gate = lambda g_ref, k: jax.nn.sigmoid(g_ref[0, :, cs].astype(F32) + bm[k:k + 1, cs])
        m = gate(g0_ref, 0) * jnp.dot(ya_ref[0], wa_ref[:, cs], preferred_element_type=F32)
        m += gate(g1_ref, 1) * jnp.dot(yb_ref[0], wb_ref[:, cs], preferred_element_type=F32)
        m += gate(g2_ref, 2) * jnp.dot(yc_ref[0], wc_ref[:, cs], preferred_element_type=F32)
        merged.append(m.astype(BF16))
    out = jnp.dot(merged[0], wo_ref[0:dh, :], preferred_element_type=F32)
    for c in range(1, nh):
        out += jnp.dot(merged[c], wo_ref[c * dh:(c + 1) * dh, :], preferred_element_type=F32)
    o_ref[0] = h_ref[0] + mod_ref[0][2:3] * _rms(out, gp_ref[...])


def _merge(ya, yb, yc, gates, b_merge, wba, wbb, wbc, wo, layer, h, mods, g_post):
    b, s, d = h.shape
    tm = _tile(s, 256)
    per_batch = mods.shape[0] > 1
    row = lambda w: pl.BlockSpec((1, tm, w), lambda bi, i: (bi, i, 0))
    gate = lambda k: pl.BlockSpec((1, tm, d), lambda bi, i: (bi, i, k))
    full = lambda a: pl.BlockSpec(a.shape, lambda bi, i: (0, 0))
    weight = lambda a: pl.BlockSpec((1,) + a.shape[1:], lambda bi, i: (layer, 0, 0), pipeline_mode=pl.Buffered(1))
    return pl.pallas_call(
        _merge_kernel,
        grid=(b, s // tm),
        in_specs=[row(ya.shape[2]), row(yb.shape[2]), row(yc.shape[2]), gate(0), gate(1), gate(2),
                  full(b_merge), weight(wba), weight(wbb), weight(wbc), weight(wo), row(d),
                  pl.BlockSpec((1, N_MOD, d), (lambda bi, i: (bi, 0, 0)) if per_batch else (lambda bi, i: (0, 0, 0))),
                  pl.BlockSpec((1, d), lambda bi, i: (0, 0))],
        out_specs=row(d),
        out_shape=jax.ShapeDtypeStruct((b, s, d), F32),
        compiler_params=_params("arbitrary", "arbitrary"),
        name="merge",
    )(ya, yb, yc, gates, gates, gates, b_merge, wba, wbb, wbc, wo, h, mods, g_post)


def _ffn_kernel(h_ref, mod_ref, gpre_ref, gpost_ref, wg_ref, wu_ref, wd_ref, o_ref, un_ref, acc_ref):
    j = pl.program_id(2)

    @pl.when(j == 0)
    def _():
        m = mod_ref[0]
        un_ref[...] = (_rms(h_ref[0], gpre_ref[...]) * (1.0 + m[4:5]) + m[3:4]).astype(BF16)
        acc_ref[...] = jnp.zeros(acc_ref.shape, F32)

    u = un_ref[...]
    hg = jnp.dot(u, wg_ref[0], preferred_element_type=F32)
    hu = jnp.dot(u, wu_ref[0], preferred_element_type=F32)
    acc_ref[...] += jnp.dot((_silu(hg) * hu).astype(BF16), wd_ref[0], preferred_element_type=F32)

    @pl.when(j == pl.num_programs(2) - 1)
    def _():
        o_ref[0] = h_ref[0] + mod_ref[0][5:6] * _rms(acc_ref[...], gpost_ref[...])


def _ffn(h, mods, g_pre, g_post, wg, wu, wd, layer):
    b, s, d = h.shape
    hidden = wg.shape[2]
    tm = _tile(s, 512)
    th = _tile(hidden, 512)
    per_batch = mods.shape[0] > 1
    return pl.pallas_call(
        _ffn_kernel,
        grid=(b, s // tm, hidden // th),
        in_specs=[pl.BlockSpec((1, tm, d), lambda bi, i, j: (bi, i, 0)),
                  pl.BlockSpec((1, N_MOD, d),
                               (lambda bi, i, j: (bi, 0, 0)) if per_batch else (lambda bi, i, j: (0, 0, 0))),
                  pl.BlockSpec((1, d), lambda bi, i, j: (0, 0)),
                  pl.BlockSpec((1, d), lambda bi, i, j: (0, 0)),
                  pl.BlockSpec((1, d, th), lambda bi, i, j: (layer, 0, j)),
                  pl.BlockSpec((1, d, th), lambda bi, i, j: (layer, 0, j)),
                  pl.BlockSpec((1, th, d), lambda bi, i, j: (layer, j, 0))],
        out_specs=pl.BlockSpec((1, tm, d), lambda bi, i, j: (bi, i, 0)),
        out_shape=jax.ShapeDtypeStruct((b, s, d), F32),
        scratch_shapes=[pltpu.VMEM((tm, d), BF16), pltpu.VMEM((tm, d), F32)],
        compiler_params=_params("arbitrary", "arbitrary", "arbitrary"),
        name="ffn",
    )(h, mods, g_pre, g_post, wg, wu, wd)


def _rope_tables(n_tokens, hd):
    freqs = hd // 4
    pos = jnp.arange(n_tokens)
    rowcol = jnp.stack([(pos // GRID_W).astype(F32), (pos % GRID_W).astype(F32)], axis=1)
    inv = ROPE_BASE ** (-jnp.arange(freqs, dtype=F32) / freqs)
    lane = jnp.arange(LANES) % hd
    axis, half, f = lane // (2 * freqs), (lane % (2 * freqs)) // freqs, lane % freqs
    ang = rowcol[:, axis] * inv[f][None, :]
    cos, sin = jnp.cos(ang), jnp.sin(ang)
    return cos, jnp.where(half == 1, sin, 0.0), jnp.where(half == 0, -sin, 0.0)


def kernel(x, c, ctx, c_ctx, w_ada, b_ada, g_pre_mix, g_post_mix, g_pre_ffn, g_post_ffn, w_in, diff_lambda, diff_subln, conv_w, conv_b, rnn_conv_w, rnn_conv_b, rg_wa, rg_ba, rg_wx, rg_bx, rg_lambda, b_merge, w_branch_a, w_branch_b, w_branch_c, w_o, w_ffn_gate, w_ffn_up, w_ffn_down):
    b, s, d = x.shape
    depth = w_ada.shape[0]
    hd = diff_lambda.shape[-1]
    diff_w = w_branch_a.shape[1]
    conv_width = conv_w.shape[-1]
    rnn_width = rnn_conv_w.shape[-1]
    nblk, bw = rg_wa.shape[2], rg_wa.shape[3]
    assert diff_w == conv_width == rnn_width and 2 * hd == LANES
    gate_col_rnn = 3

    rows = -(-(b + 1) // SUBLANES) * SUBLANES
    cc = jnp.zeros((rows, d), F32).at[:b].set(c).at[b].set(c_ctx)
    mods = _ada(cc, w_ada, b_ada.reshape(depth, 1, N_MOD * d))
    tables = _rope_tables(s, hd)
    q_scale = hd ** -0.5 * math.log2(math.e)

    vec = lambda a: a.reshape(2, nblk, 1, bw)
    w_in_b = w_in.astype(BF16)
    wba, wbb, wbc, wo = (w.astype(BF16) for w in (w_branch_a, w_branch_b, w_branch_c, w_o))
    wg, wu, wd = (w.astype(BF16) for w in (w_ffn_gate, w_ffn_up, w_ffn_down))
    n_ctx = ctx.shape[1]
    flat = lambda a: a.reshape(1, b * n_ctx, a.shape[-1])
    per_batch = lambda a: a.reshape(b, n_ctx, a.shape[-1])
    h_lat, h_ctx = x, flat(ctx)
    for l in range(depth):
        need_ctx = l < depth - 1
        lam_init = 0.8 - 0.6 * math.exp(-0.3 * l)
        li = jnp.full((1, 1), lam_init, F32)
        ml = mods[l, :b].reshape(b, N_MOD, d)
        mc = mods[l, b:b + 1].reshape(1, N_MOD, d)
        g_pre = g_pre_mix[l].reshape(1, d)
        subln = diff_subln[l].reshape(2 * hd, 1)

        qk_l, vt_l, rest_l, gates_l = _inproj(h_lat, ml, g_pre, w_in_b, l, tables, width=diff_w, q_scale=q_scale)
        qk_c, vt_c, rest_c, gates_c = _inproj(h_ctx, mc, g_pre, w_in_b, l, None, width=diff_w, q_scale=q_scale)
        qk_c, rest_c = per_batch(qk_c), per_batch(rest_c)

        ya_l = _attn(qk_l, qk_c, vt_c, qk_l, vt_l, diff_lambda[l], subln, li, hd=hd)

        prep = functools.partial(_prep, conv_w=conv_w[l], conv_b=conv_b[l].reshape(1, -1),
                                 rnn_conv_w=rnn_conv_w[l], rnn_conv_b=rnn_conv_b[l].reshape(1, -1),
                                 conv_width=conv_width, rnn_width=rnn_width)
        yb_l, xr_l = prep(rest_l)
        yb_c, xr_c = prep(rest_c)

        scan = functools.partial(_scan, wa=rg_wa[l], wx=rg_wx[l], ba=vec(rg_ba[l]), bx=vec(rg_bx[l]),
                                 lam=vec(rg_lambda[l]), gate_col=gate_col_rnn)
        zero = jnp.zeros((b, rnn_width), F32)
        hf_c, hfin_f = scan(xr_c, h0=zero, hf=None, rest=None, direction=0)
        yc_c, hfin_b = scan(xr_c, h0=zero, hf=hf_c, rest=rest_c, direction=1)
        hf_l, _ = scan(xr_l, h0=hfin_f, hf=None, rest=None, direction=0)
        yc_l, _ = scan(xr_l, h0=hfin_b, hf=hf_l, rest=rest_l, direction=1)

        g_post = g_post_mix[l].reshape(1, d)
        gf_pre, gf_post = g_pre_ffn[l].reshape(1, d), g_post_ffn[l].reshape(1, d)

        h_lat = _merge(ya_l, yb_l, yc_l, gates_l, b_merge[l], wba, wbb, wbc, wo, l, h_lat, ml, g_post)
        h_lat = _ffn(h_lat, ml, gf_pre, gf_post, wg, wu, wd, l)
        if need_ctx:
            ya_c = _attn(qk_c, qk_c, vt_c, None, None, diff_lambda[l], subln, li, hd=hd)
            h_ctx = _merge(flat(ya_c), flat(yb_c), flat(yc_c), gates_c, b_merge[l], wba, wbb, wbc, wo, l, h_ctx, mc,
                           g_post)
            h_ctx = _ffn(h_ctx, mc, gf_pre, gf_post, wg, wu, wd, l)
    return h_lat
```

```python
import functools
import math

import jax
import jax.numpy as jnp
from jax import lax
from jax.experimental import pallas as pl
from jax.experimental.pallas import tpu as pltpu

GRID_W = 64
ROPE_BASE = 10000.0
EPS = 1e-6
RG_C = 8.0
N_MOD = 6
N_BRANCH = 3
N_MID = 5
LANES = 128
SUBLANES = 8
VMEM_LIMIT_BYTES = 56 * 1024 * 1024
INPROJ_CHUNK = 512
ATTN_KV_STAGE = 512
ONES_ROWS = 16
SCAN_BLOCKS = 2
SCAN_ROW_PAD = 8

F32 = jnp.float32
BF16 = jnp.bfloat16


def _tile(n, pref):
    if n <= pref:
        return n
    t = pref - pref % SUBLANES
    while t >= SUBLANES:
        if n % t == 0:
            return t
        t -= SUBLANES
    return n


def _params(*sem):
    return pltpu.CompilerParams(dimension_semantics=sem, vmem_limit_bytes=VMEM_LIMIT_BYTES)


def _rms(x, g):
    return x * lax.rsqrt(jnp.mean(x * x, axis=-1, keepdims=True) + EPS) * g


def _silu(x):
    return x * jax.nn.sigmoid(x)


def _ada_kernel(x_ref, w_ref, b_ref, o_ref):
    sx = _silu(x_ref[...]).astype(BF16)
    o_ref[0] = jnp.dot(sx, w_ref[0].astype(BF16), preferred_element_type=F32) + b_ref[0]


def _ada(cc, w_ada, b_flat):
    depth, d, n = w_ada.shape
    rows = cc.shape[0]
    tn = _tile(n, 1024)
    return pl.pallas_call(
        _ada_kernel,
        grid=(depth, n // tn),
        in_specs=[
            pl.BlockSpec((rows, d), lambda l, j: (0, 0)),
            pl.BlockSpec((1, d, tn), lambda l, j: (l, 0, j)),
            pl.BlockSpec((1, 1, tn), lambda l, j: (l, 0, j)),
        ],
        out_specs=pl.BlockSpec((1, rows, tn), lambda l, j: (l, 0, j)),
        out_shape=jax.ShapeDtypeStruct((depth, rows, n), F32),
        compiler_params=_params("arbitrary", "arbitrary"),
        name="ada",
    )(cc, w_ada, b_flat)


def _inproj_kernel(h_ref, mod_ref, g_ref, w_ref, *rest, rope, q_scale):
    if rope:
        cos_ref, s1_ref, s2_ref, qkv_ref, vt_ref, mid_ref, gates_ref, xn_ref = rest
    else:
        qkv_ref, vt_ref, mid_ref, gates_ref, xn_ref = rest
    j = pl.program_id(2)

    @pl.when(j == 0)
    def _():
        m = mod_ref[0]
        y = _rms(h_ref[0], g_ref[...])
        xn_ref[...] = (y * (1.0 + m[1:2]) + m[0:1]).astype(BF16)

    tn = w_ref.shape[2]
    cw = INPROJ_CHUNK if tn % INPROJ_CHUNK == 0 else tn

    def chunked(epilogue):
        for c in range(tn // cw):
            epilogue(c * cw, jnp.dot(xn_ref[...], w_ref[0, :, c * cw:(c + 1) * cw], preferred_element_type=F32))

    def qk_epilogue(c0, acc):
        a = acc * jnp.where(j == 0, q_scale, 1.0)
        if rope:
            cos, s1, s2 = cos_ref[...], s1_ref[...], s2_ref[...]
            quarter = LANES // 8
            for c in range(cw // LANES):
                blk = a[:, c * LANES:(c + 1) * LANES]
                r = blk * cos + pltpu.roll(blk, quarter, 1) * s1 + pltpu.roll(blk, LANES - quarter, 1) * s2
                qkv_ref[0, :, c0 + c * LANES:c0 + (c + 1) * LANES] = r.astype(BF16)
        else:
            qkv_ref[0, :, c0:c0 + cw] = a.astype(BF16)

    def v_epilogue(c0, acc):
        vt_ref[0, c0:c0 + cw, :] = acc.T.astype(BF16)

    def mid_epilogue(c0, acc):
        mid_ref[0, :, c0:c0 + cw] = acc.astype(BF16)

    def gates_epilogue(c0, acc):
        gates_ref[0, :, c0:c0 + cw] = acc.astype(BF16)

    pl.when(j < 2)(lambda: chunked(qk_epilogue))
    pl.when(j == 2)(lambda: chunked(v_epilogue))
    pl.when((j >= 3) & (j < 3 + N_MID))(lambda: chunked(mid_epilogue))
    pl.when(j >= 3 + N_MID)(lambda: chunked(gates_epilogue))


def _inproj(h, mods, g, w, layer, tables, *, width, q_scale):
    b, s, d = h.shape
    tn = width
    n = w.shape[2]
    assert n == (3 + N_MID) * width + N_BRANCH * d and d % tn == 0
    tm = _tile(s, 1024)
    rope = tables is not None
    per_batch = mods.shape[0] > 1
    in_specs = [
        pl.BlockSpec((1, tm, d), lambda bi, i, j: (bi, i, 0)),
        pl.BlockSpec((1, N_MOD, d), (lambda bi, i, j: (bi, 0, 0)) if per_batch else (lambda bi, i, j: (0, 0, 0))),
        pl.BlockSpec((1, d), lambda bi, i, j: (0, 0)),
        pl.BlockSpec((1, d, tn), lambda bi, i, j: (layer, 0, j)),
    ]
    args = [h, mods, g, w]
    if rope:
        in_specs += [pl.BlockSpec((tm, LANES), lambda bi, i, j: (i, 0))] * 3
        args += list(tables)
    return pl.pallas_call(
        functools.partial(_inproj_kernel, rope=rope, q_scale=q_scale),
        grid=(b, s // tm, n // tn),
        in_specs=in_specs,
        out_specs=[
            pl.BlockSpec((1, tm, tn), lambda bi, i, j: (bi, i, jnp.minimum(j, 1))),
            pl.BlockSpec((1, tn, tm), lambda bi, i, j: (bi, 0, i)),
            pl.BlockSpec((1, tm, tn), lambda bi, i, j: (bi, i, jnp.clip(j - 3, 0, N_MID - 1))),
            pl.BlockSpec((1, tm, tn), lambda bi, i, j: (bi, i, jnp.maximum(j - 3 - N_MID, 0))),
        ],
        out_shape=[
            jax.ShapeDtypeStruct((b, s, 2 * width), BF16),
            jax.ShapeDtypeStruct((b, width, s), BF16),
            jax.ShapeDtypeStruct((b, s, N_MID * width), BF16),
            jax.ShapeDtypeStruct((b, s, N_BRANCH * d), BF16),
        ],
        scratch_shapes=[pltpu.VMEM((tm, d), BF16)],
        compiler_params=_params("arbitrary", "arbitrary", "arbitrary"),
        name="inproj_rope" if rope else "inproj",
    )(*args)


def _attn_kernel(*refs, heads, hd, has_lat):
    if has_lat:
        (q_ref, kc_ref, vc_ref, kl_ref, vl_ref, dl_ref, sg_ref, li_ref, o_ref,
         qs_ref, m_ref, acc_ref, s_ref) = refs
    else:
        q_ref, kc_ref, vc_ref, dl_ref, sg_ref, li_ref, o_ref, qs_ref, m_ref, acc_ref, s_ref = refs
    j = pl.program_id(2)
    last = pl.num_programs(2) - 1
    tq = q_ref.shape[1]
    hw = 2 * hd

    @pl.when(j == 0)
    def _():
        lane = lax.broadcasted_iota(jnp.int32, (tq, hw), 1)
        for h in range(heads):
            q = q_ref[0, :, h * hw:(h + 1) * hw].astype(F32)
            qs_ref[h, 0:tq] = jnp.where(lane < hd, q, 0.0).astype(BF16)
            qs_ref[h, tq:2 * tq] = jnp.where(lane >= hd, q, 0.0).astype(BF16)
        m_ref[...] = jnp.full(m_ref.shape, -jnp.inf, F32)
        acc_ref[...] = jnp.zeros(acc_ref.shape, F32)

    def step(k_ref, vt_ref):
        tk = k_ref.shape[1]
        sub = min(tk, s_ref.shape[1])
        ones = jnp.ones((ONES_ROWS, sub), BF16)
        stages = [(h, c * sub) for h in range(heads) for c in range(tk // sub)]

        def scores(t):
            h, k0 = stages[t]
            k = k_ref[0, k0:k0 + sub, h * hw:(h + 1) * hw]
            s_ref[t % 2, 0:sub, :] = lax.dot_general(k, qs_ref[h], (((1,), (1,)), ((), ())),
                                                     preferred_element_type=F32)

        scores(0)
        for t, (h, k0) in enumerate(stages):
            if t + 1 < len(stages):
                scores(t + 1)
            vt = jnp.concatenate([vt_ref[0, h * hw:(h + 1) * hw, k0:k0 + sub], ones], axis=0)
            s = s_ref[t % 2, 0:sub, :]
            m_prev = m_ref[h]
            m_new = jnp.maximum(m_prev, jnp.max(s, axis=0, keepdims=True))
            alpha = jnp.exp2(m_prev - m_new)
            p = jnp.exp2(s - m_new)
            acc_ref[h] = alpha * acc_ref[h] + jnp.dot(vt, p.astype(BF16), preferred_element_type=F32)
            m_ref[h] = m_new

    if has_lat:
        @pl.when(j == 0)
        def _():
            step(kc_ref, vc_ref)

        @pl.when(j > 0)
        def _():
            step(kl_ref, vl_ref)
    else:
        step(kc_ref, vc_ref)

    @pl.when(j == last)
    def _():
        dl = dl_ref[...]
        lam_init = li_ref[...]
        lam = (jnp.exp(jnp.sum(dl[0:1] * dl[1:2], axis=-1, keepdims=True))
               - jnp.exp(jnp.sum(dl[2:3] * dl[3:4], axis=-1, keepdims=True)) + lam_init)
        for h in range(heads):
            acc = acc_ref[h, 0:hw]
            rl = 1.0 / acc_ref[h, hw:hw + 1]
            o = acc[:, 0:tq] * rl[:, 0:tq] - (lam * rl[:, tq:2 * tq]) * acc[:, tq:2 * tq]
            o = o * lax.rsqrt(jnp.mean(o * o, axis=0, keepdims=True) + EPS) * sg_ref[...] * (1.0 - lam_init)
            o_ref[0, :, h * hw:(h + 1) * hw] = o.T.astype(BF16)


def _attn(qk_q, qk_c, vt_c, qk_l, vt_l, diff_lambda, subln, lam_init, *, hd):
    b, sq, w2 = qk_q.shape
    width = w2 // 2
    heads = width // (2 * hd)
    n_ctx = qk_c.shape[1]
    has_lat = qk_l is not None
    tq = _tile(sq, 512)
    vc_map = (lambda bi, i, j: (bi, 0, 0)) if vt_c.shape[0] == b else (lambda bi, i, j: (0, 0, bi))
    in_specs = [
        pl.BlockSpec((1, tq, width), lambda bi, i, j: (bi, i, 0)),
        pl.BlockSpec((1, n_ctx, width), lambda bi, i, j: (bi, 0, 1)),
        pl.BlockSpec((1, width, n_ctx), vc_map),
    ]
    args = [qk_q, qk_c, vt_c]
    nkv = 1
    tk = n_ctx
    if has_lat:
        sk = qk_l.shape[1]
        tk = _tile(sk, 2048)
        assert tk >= n_ctx
        nkv += sk // tk
        in_specs += [
            pl.BlockSpec((1, tk, width), lambda bi, i, j: (bi, jnp.maximum(j - 1, 0), 1)),
            pl.BlockSpec((1, width, tk), lambda bi, i, j: (bi, 0, jnp.maximum(j - 1, 0))),
        ]
        args += [qk_l, vt_l]
    in_specs += [
        pl.BlockSpec(diff_lambda.shape, lambda bi, i, j: (0, 0)),
        pl.BlockSpec((2 * hd, 1), lambda bi, i, j: (0, 0)),
        pl.BlockSpec((1, 1), lambda bi, i, j: (0, 0)),
    ]
    args += [diff_lambda, subln, lam_init]
    return pl.pallas_call(
        functools.partial(_attn_kernel, heads=heads, hd=hd, has_lat=has_lat),
        grid=(b, sq // tq, nkv),
        in_specs=in_specs,
        out_specs=pl.BlockSpec((1, tq, width), lambda bi, i, j: (bi, i, 0)),
        out_shape=jax.ShapeDtypeStruct((b, sq, width), BF16),
        scratch_shapes=[
            pltpu.VMEM((heads, 2 * tq, 2 * hd), BF16),
            pltpu.VMEM((heads, 1, 2 * tq), F32),
            pltpu.VMEM((heads, 2 * hd + ONES_ROWS, 2 * tq), F32),
            pltpu.VMEM((2, min(tk, ATTN_KV_STAGE), 2 * tq), F32),
        ],
        compiler_params=_params("arbitrary", "arbitrary", "arbitrary"),
        name="attn_lat" if has_lat else "attn_ctx",
    )(*args)


def _prep_kernel(x3_ref, x4_ref, x5_ref, x7_ref, cw_ref, cb_ref, rw_ref, rb_ref, yb_ref, xr_ref):
    s = x3_ref.shape[1]
    row = lax.broadcasted_iota(jnp.int32, (s, x3_ref.shape[2]), 0)

    def shifted(x, k):
        r = pltpu.roll(x, k % s, 0)
        return jnp.where((row >= k) & (row < s + k), r, 0.0)

    z = x4_ref[0].astype(F32) * x3_ref[0].astype(F32)
    cw = cw_ref[...]
    conv = cw[0:1] * shifted(z, 1) + cw[1:2] * z + cw[2:3] * shifted(z, -1) + cb_ref[...]
    yb_ref[0] = (x5_ref[0].astype(F32) * conv).astype(BF16)
    x = x7_ref[0].astype(F32)
    rw = rw_ref[...]
    xr_ref[0] = (rw[0:1] * shifted(x, 2) + rw[1:2] * shifted(x, 1) + rw[2:3] * x + rw[3:4] * shifted(x, -1)
                 + rb_ref[...])


def _prep(rest, conv_w, conv_b, rnn_conv_w, rnn_conv_b, *, conv_width, rnn_width):
    b, s, _ = rest.shape
    assert conv_width == rnn_width
    tc = LANES
    nct = conv_width // tc
    col = lambda k: (lambda bi, c: (bi, 0, k * nct + c))
    par = lambda rows: pl.BlockSpec((rows, tc), lambda bi, c: (0, c))
    return pl.pallas_call(
        _prep_kernel,
        grid=(b, nct),
        in_specs=[pl.BlockSpec((1, s, tc), col(0)), pl.BlockSpec((1, s, tc), col(1)),
                  pl.BlockSpec((1, s, tc), col(2)), pl.BlockSpec((1, s, tc), col(4)),
                  par(conv_w.shape[0]), par(1), par(rnn_conv_w.shape[0]), par(1)],
        out_specs=[pl.BlockSpec((1, s, tc), lambda bi, c: (bi, 0, c)),
                   pl.BlockSpec((1, s, tc), lambda bi, c: (bi, 0, c))],
        out_shape=[jax.ShapeDtypeStruct((b, s, conv_width), BF16),
                   jax.ShapeDtypeStruct((b, s, rnn_width), F32)],
        compiler_params=_params("arbitrary", "arbitrary"),
        name="conv_prep",
    )(rest, rest, rest, rest, conv_w, conv_b, rnn_conv_w, rnn_conv_b)


def _scan_kernel(*refs, reverse, finalize):
    if finalize:
        (xr_ref, wa_ref, wx_ref, ba_ref, bx_ref, lam_ref, h0_ref, hf_ref, gate_ref, out_ref, hlast_ref,
         a_s, g_s, h_s, carry) = refs
    else:
        xr_ref, wa_ref, wx_ref, ba_ref, bx_ref, lam_ref, h0_ref, out_ref, hlast_ref, a_s, g_s, h_s, carry = refs
    i = pl.program_id(1)
    nb, tc, cw = xr_ref.shape
    ng = cw // LANES

    @pl.when(i == 0)
    def _():
        carry[...] = h0_ref[...]

    pitch = tc + SCAN_ROW_PAD
    for k in range(ng):
        lanes = slice(k * LANES, (k + 1) * LANES)
        x = xr_ref[:, :, lanes].reshape(nb * tc, LANES)
        xb = x.astype(BF16)
        r = jax.nn.sigmoid(jnp.dot(xb, wa_ref[0, k].astype(BF16), preferred_element_type=F32) + ba_ref[0, k])
        gi = jax.nn.sigmoid(jnp.dot(xb, wx_ref[0, k].astype(BF16), preferred_element_type=F32) + bx_ref[0, k])
        z = -lam_ref[0, k]
        softplus = jnp.maximum(z, 0.0) + jnp.log1p(jnp.exp(-jnp.abs(z)))
        a = jnp.exp2(r * ((-RG_C * math.log2(math.e)) * softplus))
        g = jnp.sqrt(1.0 - a * a) * (gi * x)
        for bi in range(nb):
            a_s[k, bi * pitch:bi * pitch + tc, :] = a[bi * tc:(bi + 1) * tc]
            g_s[k, bi * pitch:bi * pitch + tc, :] = g[bi * tc:(bi + 1) * tc]

    def body(t, hs):
        tt = tc - 1 - t if reverse else t
        rows = pl.ds(tt, nb, stride=pitch)
        new = []
        for k in range(ng):
            h = a_s[k, rows, :] * hs[k] + g_s[k, rows, :]
            h_s[k, rows, :] = h
            new.append(h)
        return tuple(new)

    hs = lax.fori_loop(0, tc, body, tuple(carry[:, k * LANES:(k + 1) * LANES] for k in range(ng)), unroll=8)
    for k in range(ng):
        lanes = slice(k * LANES, (k + 1) * LANES)
        carry[:, lanes] = hs[k]
        hlast_ref[:, lanes] = hs[k]
        for bi in range(nb):
            hb = h_s[k, bi * pitch:bi * pitch + tc, :]
            if finalize:
                out_ref[bi, :, lanes] = (jax.nn.gelu(gate_ref[bi, :, lanes].astype(F32))
                                         * (hf_ref[bi, :, lanes] + hb)).astype(out_ref.dtype)
            else:
                out_ref[bi, :, lanes] = hb


def _scan(xr, wa, wx, ba, bx, lam, h0, hf, rest, *, direction, gate_col):
    b, s, c = xr.shape
    nblk, bw = wa.shape[1], wa.shape[2]
    assert bw == LANES and nblk * bw == c
    tc = _tile(s, 512)
    nchunk = s // tc
    reverse = direction == 1
    finalize = hf is not None
    chunk = (lambda i: nchunk - 1 - i) if reverse else (lambda i: i)
    d = direction
    ng = SCAN_BLOCKS if nblk % SCAN_BLOCKS == 0 else 1
    gw = ng * bw
    seq_spec = pl.BlockSpec((b, tc, gw), lambda n, i: (0, chunk(i), n))
    w_spec = pl.BlockSpec((1, ng, bw, bw), lambda n, i: (d, n, 0, 0))
    v_spec = pl.BlockSpec((1, ng, 1, bw), lambda n, i: (d, n, 0, 0))
    st_spec = pl.BlockSpec((b, gw), lambda n, i: (0, n))
    in_specs = [seq_spec, w_spec, w_spec, v_spec, v_spec, v_spec, st_spec]
    args = [xr, wa, wx, ba, bx, lam, h0]
    if finalize:
        in_specs += [seq_spec, pl.BlockSpec((b, tc, gw), lambda n, i: (0, chunk(i), gate_col * (nblk // ng) + n))]
        args += [hf, rest]
    return pl.pallas_call(
        functools.partial(_scan_kernel, reverse=reverse, finalize=finalize),
        grid=(nblk // ng, nchunk),
        in_specs=in_specs,
        out_specs=[seq_spec, st_spec],
        out_shape=[jax.ShapeDtypeStruct((b, s, c), BF16 if finalize else F32),
                   jax.ShapeDtypeStruct((b, c), F32)],
        scratch_shapes=[pltpu.VMEM((ng, b * (tc + SCAN_ROW_PAD), bw), F32)] * 3 + [pltpu.VMEM((b, gw), F32)],
        compiler_params=_params("arbitrary", "arbitrary"),
        name="rglru_bwd" if reverse else "rglru_fwd",
    )(*args)


def _merge_kernel(ya_ref, yb_ref, yc_ref, g0_ref, g1_ref, g2_ref, bm_ref, wa_ref, wb_ref, wc_ref, wo_ref,
                  h_ref, mod_ref, gp_ref, o_ref):
    bm = bm_ref[...]
    wa_ref, wb_ref, wc_ref, wo_ref = wa_ref.at[0], wb_ref.at[0], wc_ref.at[0], wo_ref.at[0]
    d = wo_ref.shape[0]
    nh = 2 if d % (2 * LANES) == 0 else 1
    dh = d // nh
    merged = []
    for c in range(nh):
        cs = slice(c * dh, (c + 1) * dh)
        gate = lambda g_ref, k: jax.nn.sigmoid(g_ref[0, :, cs].astype(F32) + bm[k:k + 1, cs])
        m = gate(g0_ref, 0) * jnp.dot(ya_ref[0], wa_ref[:, cs], preferred_element_type=F32)
        m += gate(g1_ref, 1) * jnp.dot(yb_ref[0], wb_ref[:, cs], preferred_element_type=F32)
        m += gate(g2_ref, 2) * jnp.dot(yc_ref[0], wc_ref[:, cs], preferred_element_type=F32)
        merged.append(m.astype(BF16))
    out = jnp.dot(merged[0], wo_ref[0:dh, :], preferred_element_type=F32)
    for c in range(1, nh):
        out += jnp.dot(merged[c], wo_ref[c * dh:(c + 1) * dh, :], preferred_element_type=F32)
    o_ref[0] = h_ref[0] + mod_ref[0][2:3] * _rms(out, gp_ref[...])


def _merge(ya, yb, yc, gates, b_merge, wba, wbb, wbc, wo, layer, h, mods, g_post):
    b, s, d = h.shape
    tm = _tile(s, 256)
    per_batch = mods.shape[0] > 1
    row = lambda w: pl.BlockSpec((1, tm, w), lambda bi, i: (bi, i, 0))
    gate = lambda k: pl.BlockSpec((1, tm, d), lambda bi, i: (bi, i, k))
    full = lambda a: pl.BlockSpec(a.shape, lambda bi, i: (0, 0))
    weight = lambda a: pl.BlockSpec((1,) + a.shape[1:], lambda bi, i: (layer, 0, 0), pipeline_mode=pl.Buffered(1))
    return pl.pallas_call(
        _merge_kernel,
        grid=(b, s // tm),
        in_specs=[row(ya.shape[2]), row(yb.shape[2]), row(yc.shape[2]), gate(0), gate(1), gate(2),
                  full(b_merge), weight(wba), weight(wbb), weight(wbc), weight(wo), row(d),
                  pl.BlockSpec((1, N_MOD, d), (lambda bi, i: (bi, 0, 0)) if per_batch else (lambda bi, i: (0, 0, 0))),
                  pl.BlockSpec((1, d), lambda bi, i: (0, 0))],
        out_specs=row(d),
        out_shape=jax.ShapeDtypeStruct((b, s, d), F32),
        compiler_params=_params("arbitrary", "arbitrary"),
        name="merge",
    )(ya, yb, yc, gates, gates, gates, b_merge, wba, wbb, wbc, wo, h, mods, g_post)


def _ffn_kernel(h_ref, mod_ref, gpre_ref, gpost_ref, wg_ref, wu_ref, wd_ref, o_ref, un_ref, acc_ref):
    j = pl.program_id(2)

    @pl.when(j == 0)
    def _():
        m = mod_ref[0]
        un_ref[...] = (_rms(h_ref[0], gpre_ref[...]) * (1.0 + m[4:5]) + m[3:4]).astype(BF16)
        acc_ref[...] = jnp.zeros(acc_ref.shape, F32)

    u = un_ref[...]
    hg = jnp.dot(u, wg_ref[0], preferred_element_type=F32)
    hu = jnp.dot(u, wu_ref[0], preferred_element_type=F32)
    acc_ref[...] += jnp.dot((_silu(hg) * hu).astype(BF16), wd_ref[0], preferred_element_type=F32)

    @pl.when(j == pl.num_programs(2) - 1)
    def _():
        o_ref[0] = h_ref[0] + mod_ref[0][5:6] * _rms(acc_ref[...], gpost_ref[...])


def _ffn(h, mods, g_pre, g_post, wg, wu, wd, layer):
    b, s, d = h.shape
    hidden = wg.shape[2]
    tm = _tile(s, 512)
    th = _tile(hidden, 512)
    per_batch = mods.shape[0] > 1
    return pl.pallas_call(
        _ffn_kernel,
        grid=(b, s // tm, hidden // th),
        in_specs=[pl.BlockSpec((1, tm, d), lambda bi, i, j: (bi, i, 0)),
                  pl.BlockSpec((1, N_MOD, d),
                               (lambda bi, i, j: (bi, 0, 0)) if per_batch else (lambda bi, i, j: (0, 0, 0))),
                  pl.BlockSpec((1, d), lambda bi, i, j: (0, 0)),
                  pl.BlockSpec((1, d), lambda bi, i, j: (0, 0)),
                  pl.BlockSpec((1, d, th), lambda bi, i, j: (layer, 0, j)),
                  pl.BlockSpec((1, d, th), lambda bi, i, j: (layer, 0, j)),
                  pl.BlockSpec((1, th, d), lambda bi, i, j: (layer, j, 0))],
        out_specs=pl.BlockSpec((1, tm, d), lambda bi, i, j: (bi, i, 0)),
        out_shape=jax.ShapeDtypeStruct((b, s, d), F32),
        scratch_shapes=[pltpu.VMEM((tm, d), BF16), pltpu.VMEM((tm, d), F32)],
        compiler_params=_params("arbitrary", "arbitrary", "arbitrary"),
        name="ffn",
    )(h, mods, g_pre, g_post, wg, wu, wd)


def _rope_tables(n_tokens, hd):
    freqs = hd // 4
    pos = jnp.arange(n_tokens)
    rowcol = jnp.stack([(pos // GRID_W).astype(F32), (pos % GRID_W).astype(F32)], axis=1)
    inv = ROPE_BASE ** (-jnp.arange(freqs, dtype=F32) / freqs)
    lane = jnp.arange(LANES) % hd
    axis, half, f = lane // (2 * freqs), (lane % (2 * freqs)) // freqs, lane % freqs
    ang = rowcol[:, axis] * inv[f][None, :]
    cos, sin = jnp.cos(ang), jnp.sin(ang)
    return cos, jnp.where(half == 1, sin, 0.0), jnp.where(half == 0, -sin, 0.0)


def kernel(x, c, ctx, c_ctx, w_ada, b_ada, g_pre_mix, g_post_mix, g_pre_ffn, g_post_ffn, w_in, diff_lambda, diff_subln, conv_w, conv_b, rnn_conv_w, rnn_conv_b, rg_wa, rg_ba, rg_wx, rg_bx, rg_lambda, b_merge, w_branch_a, w_branch_b, w_branch_c, w_o, w_ffn_gate, w_ffn_up, w_ffn_down):
    b, s, d = x.shape
    depth = w_ada.shape[0]
    hd = diff_lambda.shape[-1]
    diff_w = w_branch_a.shape[1]
    conv_width = conv_w.shape[-1]
    rnn_width = rnn_conv_w.shape[-1]
    nblk, bw = rg_wa.shape[2], rg_wa.shape[3]
    assert diff_w == conv_width == rnn_width and 2 * hd == LANES
    gate_col_rnn = 3

    rows = -(-(b + 1) // SUBLANES) * SUBLANES
    cc = jnp.zeros((rows, d), F32).at[:b].set(c).at[b].set(c_ctx)
    mods = _ada(cc, w_ada, b_ada.reshape(depth, 1, N_MOD * d))
    tables = _rope_tables(s, hd)
    q_scale = hd ** -0.5 * math.log2(math.e)

    vec = lambda a: a.reshape(2, nblk, 1, bw)
    w_in_b = w_in.astype(BF16)
    wba, wbb, wbc, wo = (w.astype(BF16) for w in (w_branch_a, w_branch_b, w_branch_c, w_o))
    wg, wu, wd = (w.astype(BF16) for w in (w_ffn_gate, w_ffn_up, w_ffn_down))
    n_ctx = ctx.shape[1]
    flat = lambda a: a.reshape(1, b * n_ctx, a.shape[-1])
    per_batch = lambda a: a.reshape(b, n_ctx, a.shape[-1])
    h_lat, h_ctx = x, flat(ctx)
    for l in range(depth):
        need_ctx = l < depth - 1
        lam_init = 0.8 - 0.6 * math.exp(-0.3 * l)
        li = jnp.full((1, 1), lam_init, F32)
        ml = mods[l, :b].reshape(b, N_MOD, d)
        mc = mods[l, b:b + 1].reshape(1, N_MOD, d)
        g_pre = g_pre_mix[l].reshape(1, d)
        subln = diff_subln[l].reshape(2 * hd, 1)

        qk_l, vt_l, rest_l, gates_l = _inproj(h_lat, ml, g_pre, w_in_b, l, tables, width=diff_w, q_scale=q_scale)
        qk_c, vt_c, rest_c, gates_c = _inproj(h_ctx, mc, g_pre, w_in_b, l, None, width=diff_w, q_scale=q_scale)
        qk_c, rest_c = per_batch(qk_c), per_batch(rest_c)

        ya_l = _attn(qk_l, qk_c, vt_c, qk_l, vt_l, diff_lambda[l], subln, li, hd=hd)

        prep = functools.partial(_prep, conv_w=conv_w[l], conv_b=conv_b[l].reshape(1, -1),
                                 rnn_conv_w=rnn_conv_w[l], rnn_conv_b=rnn_conv_b[l].reshape(1, -1),
                                 conv_width=conv_width, rnn_width=rnn_width)
        yb_l, xr_l = prep(rest_l)
        yb_c, xr_c = prep(rest_c)

        scan = functools.partial(_scan, wa=rg_wa[l], wx=rg_wx[l], ba=vec(rg_ba[l]), bx=vec(rg_bx[l]),
                                 lam=vec(rg_lambda[l]), gate_col=gate_col_rnn)
        zero = jnp.zeros((b, rnn_width), F32)
        hf_c, hfin_f = scan(xr_c, h0=zero, hf=None, rest=None, direction=0)
        yc_c, hfin_b = scan(xr_c, h0=zero, hf=hf_c, rest=rest_c, direction=1)
        hf_l, _ = scan(xr_l, h0=hfin_f, hf=None, rest=None, direction=0)
        yc_l, _ = scan(xr_l, h0=hfin_b, hf=hf_l, rest=rest_l, direction=1)

        g_post = g_post_mix[l].reshape(1, d)
        gf_pre, gf_post = g_pre_ffn[l].reshape(1, d), g_post_ffn[l].reshape(1, d)

        h_lat = _merge(ya_l, yb_l, yc_l, gates_l, b_merge[l], wba, wbb, wbc, wo, l, h_lat, ml, g_post)
        h_lat = _ffn(h_lat, ml, gf_pre, gf_post, wg, wu, wd, l)
        if need_ctx:
            ya_c = _attn(qk_c, qk_c, vt_c, None, None, diff_lambda[l], subln, li, hd=hd)
            h_ctx = _merge(flat(ya_c), flat(yb_c), flat(yc_c), gates_c, b_merge[l], wba, wbb, wbc, wo, l, h_ctx, mc,
                           g_post)
            h_ctx = _ffn(h_ctx, mc, gf_pre, gf_post, wg, wu, wd, l)
    return h_lat
```
